```python
import jax
import jax.numpy as jnp
from jax import lax
import numpy as np

D_MODEL = 1024
BATCH = 32
SEQ = 2048
DEPTH = 1

GRID_W = 64
CTX_LEN = 256
EPS = 1e-6
N_MOD = 6

ATTN_HEADS = 8
ATTN_KV_HEADS = 2
ATTN_HEAD_DIM = 64
ATTN_WIDTH = ATTN_HEADS * ATTN_HEAD_DIM
KV_WIDTH = ATTN_KV_HEADS * ATTN_HEAD_DIM
Q_BLOCK = 128
ROPE_THETA = 10000.0

MLSTM_HEADS = 4
MLSTM_HEAD_DIM = 128
MLSTM_WIDTH = MLSTM_HEADS * MLSTM_HEAD_DIM
MLSTM_CHUNK = 64
CONV_W = 3

N_EXPERTS = 256
TOP_K = 8
N_GROUPS = 8
TOPK_GROUPS = 4
EXPERT_FF = 256
SHARED_FF = 256
ROUTED_SCALE = 2.5
MOE_BLOCK = 128

IN_SPLITS = (ATTN_WIDTH, KV_WIDTH, KV_WIDTH, MLSTM_WIDTH, MLSTM_WIDTH, MLSTM_WIDTH, MLSTM_WIDTH,
             2 * MLSTM_HEADS, 2 * MLSTM_HEADS, D_MODEL, D_MODEL)
IN_WIDTH = sum(IN_SPLITS)

kernel_name = "hybrid_gqa_mlstm_moe_dit_layer"


def rms_norm(x, g):
    xf = x.astype(jnp.float32)
    y = xf * lax.rsqrt(jnp.mean(xf * xf, axis=-1, keepdims=True) + EPS)
    return (y * g.astype(jnp.float32)).astype(x.dtype)


def modulate(h, shift, scale):
    return h * (1.0 + scale) + shift


def split_in(p):
    idx = np.cumsum(IN_SPLITS)[:-1].tolist()
    return jnp.split(p, idx, axis=-1)


def axial_rope_tables(n):
    rows = n // GRID_W
    row = jnp.repeat(jnp.arange(rows, dtype=jnp.int32), GRID_W)
    col = jnp.tile(jnp.arange(GRID_W, dtype=jnp.int32), rows)
    n_freq = ATTN_HEAD_DIM // 4
    inv_freq = ROPE_THETA ** (-jnp.arange(n_freq, dtype=jnp.float32) / n_freq)
    pos = jnp.stack([row, col], axis=-1).astype(jnp.float32)
    ang = pos[:, :, None] * inv_freq
    return jnp.cos(ang), jnp.sin(ang)


def apply_axial_rope(x, cos, sin):
    B, n, H, dh = x.shape
    xr = x.reshape(B, n, H, 2, 2, dh // 4)
    x1, x2 = xr[..., 0, :], xr[..., 1, :]
    c = cos[None, :, None].astype(x.dtype)
    s = sin[None, :, None].astype(x.dtype)
    return jnp.stack([x1 * c - x2 * s, x2 * c + x1 * s], axis=-2).reshape(B, n, H, dh)


def gqa_attend(q, keys, vals):
    B, Lq, Hq, dh = q.shape
    Hkv = keys.shape[2]
    qg = q.reshape(B, Lq, Hkv, Hq // Hkv, dh)
    s = jnp.einsum('bqhgd,bkhd->bhgqk', qg, keys, preferred_element_type=jnp.float32) * (dh ** -0.5)
    p = jax.nn.softmax(s, axis=-1).astype(vals.dtype)
    o = jnp.einsum('bhgqk,bkhd->bqhgd', p, vals)
    return o.reshape(B, Lq, Hq * dh)


def attention_blocks(q, keys, vals):
    B, S, Hq, dh = q.shape
    nb = S // Q_BLOCK
    qb = jnp.moveaxis(q.reshape(B, nb, Q_BLOCK, Hq, dh), 1, 0)
    out = lax.map(lambda blk: gqa_attend(blk, keys, vals), qb)
    return jnp.moveaxis(out, 0, 1).reshape(B, S, Hq * dh)


def short_conv(a, w, b):
    pad = CONV_W // 2
    n = a.shape[1]
    ap = jnp.pad(a, ((0, 0), (pad, pad), (0, 0)))
    return sum(ap[:, j:j + n] * w[j] for j in range(CONV_W)) + b


def mlstm_inputs(mq, mk, mv, mi, mf, conv_w, conv_b, b_igate, b_fgate):
    B, n, _ = mq.shape
    qk = jax.nn.silu(short_conv(jnp.concatenate([mq, mk], axis=-1), conv_w, conv_b)).astype(jnp.float32)
    q, k = jnp.split(qk, 2, axis=-1)

    def heads(a):
        return a.reshape(B, n, MLSTM_HEADS, MLSTM_HEAD_DIM).transpose(0, 2, 1, 3)

    def gates(g, bias):
        return (g.astype(jnp.float32) + bias.astype(jnp.float32)).reshape(B, n, 2, MLSTM_HEADS).transpose(2, 0, 3, 1)

    q = heads(q) * (MLSTM_HEAD_DIM ** -0.5)
    k = heads(k)
    v = heads(mv.astype(jnp.float32))
    log_i = gates(mi, b_igate)
    log_f = jax.nn.log_sigmoid(gates(mf, b_fgate))
    return q, k, v, log_i, log_f


def mlstm_scan(q, k, v, log_i, log_f, state):
    B, H, n, dk = q.shape
    dv = v.shape[-1]
    nc = n // MLSTM_CHUNK

    def to_chunks(a):
        return jnp.moveaxis(a.reshape(B, H, nc, MLSTM_CHUNK, *a.shape[3:]), 2, 0)

    xs = (to_chunks(q), to_chunks(k), to_chunks(v), to_chunks(log_i), to_chunks(log_f))
    lower = jnp.tril(jnp.ones((MLSTM_CHUNK, MLSTM_CHUNK), dtype=bool))

    def step(carry, inp):
        C, nv, m = carry
        qc, kc, vc, li, lf = inp
        b = jnp.cumsum(lf, axis=-1)
        dmat = b[..., :, None] - b[..., None, :] + li[..., None, :]
        dmat = jnp.where(lower, dmat, -jnp.inf)
        inter = b + m[..., None]
        m_q = jnp.maximum(inter, jnp.max(dmat, axis=-1))
        w_intra = jnp.exp(dmat - m_q[..., None])
        w_inter = jnp.exp(inter - m_q)
        qk = jnp.einsum('bhjd,bhsd->bhjs', qc, kc) * w_intra
        num = w_inter[..., None] * jnp.einsum('bhvd,bhjd->bhjv', C, qc) + jnp.einsum('bhjs,bhsv->bhjv', qk, vc)
        den = w_inter * jnp.einsum('bhd,bhjd->bhj', nv, qc) + jnp.sum(qk, axis=-1)
        h = num / jnp.maximum(jnp.abs(den), jnp.exp(-m_q))[..., None]
        bL = b[..., -1]
        dec = bL[..., None] - b + li
        m_new = jnp.maximum(bL + m, jnp.max(dec, axis=-1))
        w_s = jnp.exp(dec - m_new[..., None])
        w_old = jnp.exp(bL + m - m_new)
        C_new = w_old[..., None, None] * C + jnp.einsum('bhsv,bhsd->bhvd', vc * w_s[..., None], kc)
        n_new = w_old[..., None] * nv + jnp.einsum('bhs,bhsd->bhd', w_s, kc)
        return (C_new, n_new, m_new), h

    state, hs = lax.scan(step, state, xs)
    h = jnp.moveaxis(hs, 0, 2).reshape(B, H, n, dv)
    return h, state


def mlstm_output(h, o, g):
    B, H, n, dv = h.shape
    hn = rms_norm(h, g.reshape(MLSTM_HEADS, 1, MLSTM_HEAD_DIM))
    hn = hn.transpose(0, 2, 1, 3).reshape(B, n, H * dv)
    return (hn * jax.nn.sigmoid(o.astype(jnp.float32))).astype(o.dtype)


def mlstm_branch(xin, cin, need_ctx, conv_w, conv_b, b_igate, b_fgate, g_mlstm):
    qx, kx, vx, lix, lfx = mlstm_inputs(xin[0], xin[1], xin[2], xin[4], xin[5], conv_w, conv_b, b_igate, b_fgate)
    qc, kc, vc, lic, lfc = mlstm_inputs(cin[0], cin[1], cin[2], cin[4], cin[5], conv_w, conv_b, b_igate, b_fgate)
    B = qx.shape[0]
    state0 = (jnp.zeros((B, MLSTM_HEADS, MLSTM_HEAD_DIM, MLSTM_HEAD_DIM), jnp.float32),
              jnp.zeros((B, MLSTM_HEADS, MLSTM_HEAD_DIM), jnp.float32),
              jnp.zeros((B, MLSTM_HEADS), jnp.float32))

    def fl(a):
        return jnp.flip(a, axis=2)

    hc_f, st_f = mlstm_scan(qc, kc, vc, lic[0], lfc[0], state0)
    hc_b, st_b = mlstm_scan(fl(qc), fl(kc), fl(vc), fl(lic[1]), fl(lfc[1]), state0)
    hx_f, _ = mlstm_scan(qx, kx, vx, lix[0], lfx[0], st_f)
    hx_b, _ = mlstm_scan(fl(qx), fl(kx), fl(vx), fl(lix[1]), fl(lfx[1]), st_b)
    out_x = mlstm_output(hx_f + fl(hx_b), xin[3], g_mlstm)
    out_c = mlstm_output(hc_f + fl(hc_b), cin[3], g_mlstm) if need_ctx else None
    return out_x, out_c


def merge_branches(a_out, m_out, ga, gm, w_br_attn, w_br_mlstm, w_out):
    y = jax.nn.sigmoid(ga) * (a_out @ w_br_attn) + jax.nn.sigmoid(gm) * (m_out @ w_br_mlstm)
    return y @ w_out


def token_mixers(hx, hc, need_ctx, cos, sin, w_in, g_q, g_k, conv_w, conv_b, b_igate, b_fgate, g_mlstm,
                 w_br_attn, w_br_mlstm, w_out):
    qx, kx, vx, mqx, mkx, mvx, mox, mix, mfx, gax, gmx = split_in(hx @ w_in)
    qc, kc, vc, mqc, mkc, mvc, moc, mic, mfc, gac, gmc = split_in(hc @ w_in)

    def qk_heads(a, H, g):
        return rms_norm(a.reshape(*a.shape[:-1], H, ATTN_HEAD_DIM), g)

    def v_heads(a):
        return a.reshape(*a.shape[:-1], ATTN_KV_HEADS, ATTN_HEAD_DIM)

    qx = apply_axial_rope(qk_heads(qx, ATTN_HEADS, g_q), cos, sin)
    kx = apply_axial_rope(qk_heads(kx, ATTN_KV_HEADS, g_k), cos, sin)
    kc = qk_heads(kc, ATTN_KV_HEADS, g_k)
    vc = v_heads(vc)
    keys = jnp.concatenate([kc, kx], axis=1)
    vals = jnp.concatenate([vc, v_heads(vx)], axis=1)
    ax = attention_blocks(qx, keys, vals)

    mx, mc = mlstm_branch((mqx, mkx, mvx, mox, mix, mfx), (mqc, mkc, mvc, moc, mic, mfc), need_ctx,
                          conv_w, conv_b, b_igate, b_fgate, g_mlstm)
    yx = merge_branches(ax, mx, gax, gmx, w_br_attn, w_br_mlstm, w_out)
    if need_ctx:
        ac = gqa_attend(qk_heads(qc, ATTN_HEADS, g_q), kc, vc)
        yc = merge_branches(ac, mc, gac, gmc, w_br_attn, w_br_mlstm, w_out)
    else:
        yc = None
    return yx, yc


def swiglu(h, wg, wu, wd):
    return (jax.nn.silu(h @ wg) * (h @ wu)) @ wd


def routed_experts(h, eidx, gate_w, w_gate, w_up, w_down):
    T, D = h.shape
    A = T * TOP_K
    flat_e = eidx.reshape(-1)
    flat_tok = jnp.repeat(jnp.arange(T, dtype=jnp.int32), TOP_K)
    flat_w = gate_w.reshape(-1)
    order = jnp.argsort(flat_e)
    se, stok, sw = flat_e[order], flat_tok[order], flat_w[order]
    counts = jax.ops.segment_sum(jnp.ones((A,), jnp.int32), flat_e, num_segments=N_EXPERTS)
    padded = (counts + MOE_BLOCK - 1) // MOE_BLOCK * MOE_BLOCK
    pend = jnp.cumsum(padded)
    pstart = pend - padded
    ustart = jnp.cumsum(counts) - counts
    dest = pstart[se] + (jnp.arange(A, dtype=jnp.int32) - ustart[se])
    n_blocks = (A + N_EXPERTS * (MOE_BLOCK - 1) + MOE_BLOCK - 1) // MOE_BLOCK
    P = n_blocks * MOE_BLOCK
    slot_tok = jnp.zeros((P,), jnp.int32).at[dest].set(stok)
    slot_w = jnp.zeros((P,), jnp.float32).at[dest].set(sw)
    block_e = jnp.minimum(jnp.searchsorted(pend, jnp.arange(n_blocks, dtype=jnp.int32) * MOE_BLOCK, side='right'),
                          N_EXPERTS - 1)

    def body(y, blk):
        tok_b, w_b, e = blk
        xb = h[tok_b]
        hid = jax.nn.silu(xb @ w_gate[e]) * (xb @ w_up[e])
        out = (hid @ w_down[e]) * w_b[:, None].astype(h.dtype)
        return y.at[tok_b].add(out), None

    y, _ = lax.scan(body, jnp.zeros((T, D), h.dtype),
                    (slot_tok.reshape(n_blocks, MOE_BLOCK), slot_w.reshape(n_blocks, MOE_BLOCK), block_e))
    return y


def moe(h, w_router, e_bias, w_exp_gate, w_exp_up, w_exp_down, w_sh_gate, w_sh_up, w_sh_down):
    T = h.shape[0]
    scores = jax.nn.sigmoid(jnp.dot(h, w_router, preferred_element_type=jnp.float32))
    sel = (scores + e_bias.astype(jnp.float32)).reshape(T, N_GROUPS, N_EXPERTS // N_GROUPS)
    group_score = jnp.sum(lax.top_k(sel, 2)[0], axis=-1)
    _, gidx = lax.top_k(group_score, TOPK_GROUPS)
    gmask = jnp.any(gidx[..., None] == jnp.arange(N_GROUPS, dtype=gidx.dtype), axis=-2)
    masked = jnp.where(gmask[..., None], sel, -jnp.inf).reshape(T, N_EXPERTS)
    _, eidx = lax.top_k(masked, TOP_K)
    w = jnp.take_along_axis(scores, eidx, axis=-1)
    w = w / jnp.sum(w, axis=-1, keepdims=True) * ROUTED_SCALE
    return routed_experts(h, eidx, w, w_exp_gate, w_exp_up, w_exp_down) + swiglu(h, w_sh_gate, w_sh_up, w_sh_down)


def setup_inputs(seed: int = 0) -> dict:
    key = jax.random.key(seed)
    ks = jax.random.split(key, 32)
    L, D, E = DEPTH, D_MODEL, N_EXPERTS

    def nrm(k, shape, s):
        return jax.random.normal(k, shape, jnp.float32) * s

    def gain(k, shape):
        return 1.0 + nrm(k, shape, 0.05)

    return {
        "x": nrm(ks[0], (BATCH, SEQ, D), 1.0),
        "c": nrm(ks[1], (BATCH, D), 1.0),
        "ctx": nrm(ks[2], (BATCH, CTX_LEN, D), 1.0),
        "c_ctx": nrm(ks[3], (D,), 1.0),
        "w_ada": nrm(ks[4], (L, D, N_MOD * D), 0.5 * D ** -0.5),
        "b_ada": nrm(ks[5], (L, N_MOD * D), 0.02),
        "g_pre_mix": gain(ks[6], (L, D)),
        "g_post_mix": gain(ks[7], (L, D)),
        "g_pre_ffn": gain(ks[8], (L, D)),
        "g_post_ffn": gain(ks[9], (L, D)),
        "w_in": nrm(ks[10], (L, D, IN_WIDTH), D ** -0.5),
        "g_q": gain(ks[11], (L, ATTN_HEAD_DIM)),
        "g_k": gain(ks[12], (L, ATTN_HEAD_DIM)),
        "conv_w": nrm(ks[13], (L, CONV_W, 2 * MLSTM_WIDTH), CONV_W ** -0.5),
        "conv_b": nrm(ks[14], (L, 2 * MLSTM_WIDTH), 0.02),
        "b_igate": nrm(ks[15], (L, 2 * MLSTM_HEADS), 0.1),
        "b_fgate": jnp.tile(jnp.linspace(3.0, 6.0, MLSTM_HEADS), 2)[None, :] + nrm(ks[16], (L, 2 * MLSTM_HEADS), 0.1),
        "g_mlstm": gain(ks[17], (L, MLSTM_WIDTH)),
        "w_br_attn": nrm(ks[18], (L, ATTN_WIDTH, D), ATTN_WIDTH ** -0.5),
        "w_br_mlstm": nrm(ks[19], (L, MLSTM_WIDTH, D), MLSTM_WIDTH ** -0.5),
        "w_out": nrm(ks[20], (L, D, D), D ** -0.5),
        "w_router": nrm(ks[21], (L, D, E), D ** -0.5),
        "e_bias": nrm(ks[22], (L, E), 0.01),
        "w_exp_gate": nrm(ks[23], (L, E, D, EXPERT_FF), D ** -0.5),
        "w_exp_up": nrm(ks[24], (L, E, D, EXPERT_FF), D ** -0.5),
        "w_exp_down": nrm(ks[25], (L, E, EXPERT_FF, D), EXPERT_FF ** -0.5),
        "w_sh_gate": nrm(ks[26], (L, D, SHARED_FF), D ** -0.5),
        "w_sh_up": nrm(ks[27], (L, D, SHARED_FF), D ** -0.5),
        "w_sh_down": nrm(ks[28], (L, SHARED_FF, D), SHARED_FF ** -0.5),
    }


def reference(x, c, ctx, c_ctx, w_ada, b_ada, g_pre_mix, g_post_mix, g_pre_ffn, g_post_ffn, w_in, g_q, g_k,
              conv_w, conv_b, b_igate, b_fgate, g_mlstm, w_br_attn, w_br_mlstm, w_out, w_router, e_bias,
              w_exp_gate, w_exp_up, w_exp_down, w_sh_gate, w_sh_up, w_sh_down):
    B, S, D = x.shape
    CL = ctx.shape[1]
    cos, sin = axial_rope_tables(S)
    for l in range(DEPTH):
        need_ctx = l < DEPTH - 1
        mod_x = (jax.nn.silu(c) @ w_ada[l] + b_ada[l])[:, None, :]
        mod_c = jax.nn.silu(c_ctx) @ w_ada[l] + b_ada[l]
        sh1, sc1, g1, sh2, sc2, g2 = jnp.split(mod_x, N_MOD, axis=-1)
        sh1c, sc1c, g1c, sh2c, sc2c, g2c = jnp.split(mod_c, N_MOD, axis=-1)

        hx = modulate(rms_norm(x, g_pre_mix[l]), sh1, sc1)
        hc = modulate(rms_norm(ctx, g_pre_mix[l]), sh1c, sc1c)
        yx, yc = token_mixers(hx, hc, need_ctx, cos, sin, w_in[l], g_q[l], g_k[l], conv_w[l], conv_b[l],
                              b_igate[l], b_fgate[l], g_mlstm[l], w_br_attn[l], w_br_mlstm[l], w_out[l])
        x = x + g1 * rms_norm(yx, g_post_mix[l])

        moe_w = (w_router[l], e_bias[l], w_exp_gate[l], w_exp_up[l], w_exp_down[l],
                 w_sh_gate[l], w_sh_up[l], w_sh_down[l])
        fin_x = modulate(rms_norm(x, g_pre_ffn[l]), sh2, sc2).reshape(B * S, D)
        if need_ctx:
            ctx = ctx + g1c * rms_norm(yc, g_post_mix[l])
            fin_c = modulate(rms_norm(ctx, g_pre_ffn[l]), sh2c, sc2c).reshape(B * CL, D)
            f = moe(jnp.concatenate([fin_x, fin_c], axis=0), *moe_w)
            fx = f[:B * S].reshape(B, S, D)
            ctx = ctx + g2c * rms_norm(f[B * S:].reshape(B, CL, D), g_post_ffn[l])
        else:
            fx = moe(fin_x, *moe_w).reshape(B, S, D)
        x = x + g2 * rms_norm(fx, g_post_ffn[l])
    return x
```

```python
import functools

import numpy as np
import jax
import jax.numpy as jnp
from jax import lax
from jax.experimental import pallas as pl
from jax.experimental.pallas import tpu as pltpu

F32 = jnp.float32
BF16 = jnp.bfloat16
I32 = jnp.int32

D_MODEL = 1024
GRID_W = 64
EPS = 1e-6
N_MOD = 6
ATTN_HEADS = 8
ATTN_KV_HEADS = 2
ATTN_HEAD_DIM = 64
ATTN_WIDTH = ATTN_HEADS * ATTN_HEAD_DIM
KV_WIDTH = ATTN_KV_HEADS * ATTN_HEAD_DIM
ROPE_THETA = 10000.0
MLSTM_HEADS = 4
MLSTM_HEAD_DIM = 128
MLSTM_WIDTH = MLSTM_HEADS * MLSTM_HEAD_DIM
CONV_W = 3
N_EXPERTS = 256
TOP_K = 8
N_GROUPS = 8
TOPK_GROUPS = 4
GROUP_SIZE = N_EXPERTS // N_GROUPS
EXPERT_FF = 256
SHARED_FF = 256
ROUTED_SCALE = 2.5

LANES = 128
SUBLANES = 8
VMEM_LIMIT = 56 * 1024 * 1024

MLSTM_L = 128
EXPERT_BM = 256
NEG_INF = float("-inf")


def _cparams(sem):
    return pltpu.CompilerParams(dimension_semantics=sem, vmem_limit_bytes=VMEM_LIMIT)


def _split3(a):
    hi = a.astype(BF16)
    r1 = a - hi.astype(F32)
    mid = r1.astype(BF16)
    lo = (r1 - mid.astype(F32)).astype(BF16)
    return hi, mid, lo


def _dot(a, b):
    return jnp.dot(a, b, preferred_element_type=F32)


def _dot_nt(a, b):
    return lax.dot_general(a, b, (((1,), (1,)), ((), ())), preferred_element_type=F32)


def _dot_tn(a, b):
    return lax.dot_general(a, b, (((0,), (0,)), ((), ())), preferred_element_type=F32)


def _dot3_right(a_f32, t_bf16):
    hi, mid, lo = _split3(a_f32)
    return _dot(hi, t_bf16) + _dot(mid, t_bf16) + _dot(lo, t_bf16)


def _dot3_left(t_bf16, a_f32):
    hi, mid, lo = _split3(a_f32)
    return _dot(t_bf16, hi) + _dot(t_bf16, mid) + _dot(t_bf16, lo)


def _sigmoid(x):
    return 1.0 / (1.0 + jnp.exp(-x))


def _silu(x):
    return x * _sigmoid(x)


def _log_sigmoid(x):
    return jnp.minimum(x, 0.0) - jnp.log(1.0 + jnp.exp(-jnp.abs(x)))


def _rms_rows(x, g):
    return x * lax.rsqrt(jnp.mean(x * x, axis=-1, keepdims=True) + EPS) * g


def _ada_kernel(c_ref, w_ref, b_ref, o_ref):
    a = _silu(c_ref[...]).astype(BF16)
    o_ref[...] = _dot(a, w_ref[...].astype(BF16)) + b_ref[...]


def _ada_call(cc, w_ada, b_ada):
    nb, d = cc.shape
    n = w_ada.shape[1]
    tn = 1536
    return pl.pallas_call(
        _ada_kernel,
        grid=(n // tn,),
        in_specs=[pl.BlockSpec((nb, d), lambda j: (0, 0)),
                  pl.BlockSpec((d, tn), lambda j: (0, j)),
                  pl.BlockSpec((1, tn), lambda j: (0, j))],
        out_specs=pl.BlockSpec((nb, tn), lambda j: (0, j)),
        out_shape=jax.ShapeDtypeStruct((nb, n), F32),
        compiler_params=_cparams(("parallel",)),
        name="ada",
    )(cc, w_ada, b_ada)


def _head_norm(x, g, bd):
    ss = _dot3_right(x * x, bd)
    return x * lax.rsqrt(ss * (1.0 / ATTN_HEAD_DIM) + EPS) * g


def _rope(x, cos, sin_signed):
    w = x.shape[1]
    q = ATTN_HEAD_DIM // 4
    lane = lax.broadcasted_iota(I32, x.shape, 1)
    first = (lane % (2 * q)) < q
    partner = jnp.where(first, pltpu.roll(x, w - q, 1), pltpu.roll(x, q, 1))
    return x * cos + partner * sin_signed


def _inproj_kernel(is_ctx, *refs):
    if is_ctx:
        (x_ref, sh_ref, sc_ref, gpre_ref, wkv_ref, wm_ref, wif_ref, wift_ref, bif_ref, bift_ref,
         gk_ref, bd_ref, k_out, v_out, mqk_out, mv_out, gc_out, gt_out) = refs
    else:
        (x_ref, sh_ref, sc_ref, gpre_ref, wkv_ref, wm_ref, wif_ref, wift_ref, bif_ref, bift_ref,
         gk_ref, bd_ref, wq_ref, wo_ref, wg_ref, gq_ref, cos_ref, sin_ref,
         k_out, v_out, mqk_out, mv_out, gc_out, gt_out, q_out, mo_out, gg_out) = refs
    x = x_ref[...]
    h = _rms_rows(x, gpre_ref[...])
    h = h * (1.0 + sc_ref[...]) + sh_ref[...]
    hb = h.astype(BF16)

    kv = _dot(hb, wkv_ref[...])
    k = _head_norm(kv[:, :KV_WIDTH], gk_ref[...], bd_ref[:KV_WIDTH, :KV_WIDTH])
    if not is_ctx:
        k = _rope(k, cos_ref[:, :KV_WIDTH], sin_ref[:, :KV_WIDTH])
    k_out[...] = k.astype(BF16)
    v_out[...] = kv[:, KV_WIDTH:].astype(BF16)

    m = _dot(hb, wm_ref[...])
    mqk_out[...] = m[:, :2 * MLSTM_WIDTH].astype(BF16)
    mv_out[...] = m[:, 2 * MLSTM_WIDTH:].astype(BF16)

    gc = _dot(hb, wif_ref[...]) + bif_ref[...]
    lane = lax.broadcasted_iota(I32, gc.shape, 1)
    gc_out[...] = jnp.where(lane < 2 * MLSTM_HEADS, gc, _log_sigmoid(gc))
    gt = _dot_nt(wift_ref[...], hb) + bift_ref[...]
    row = lax.broadcasted_iota(I32, gt.shape, 0)
    gt_out[...] = jnp.where(row < 2 * MLSTM_HEADS, gt, _log_sigmoid(gt))

    if not is_ctx:
        q = _dot(hb, wq_ref[...])
        q = _head_norm(q, gq_ref[...], bd_ref[...])
        q = _rope(q, cos_ref[...], sin_ref[...]) * (ATTN_HEAD_DIM ** -0.5)
        q_out[...] = q.astype(BF16)
        mo_out[...] = _dot(hb, wo_ref[...]).astype(BF16)
        gg_out[...] = _dot(hb, wg_ref[...]).astype(BF16)


def _inproj_call(is_ctx, xin, mod4, mod_row_of_batch, gpre, wts, tables, tm):
    B, n, D = xin.shape
    ns = n // tm
    grid = (ns, B)

    def tok(w):
        return pl.BlockSpec((None, tm, w), lambda s, b: (b, s, 0))

    def full(a):
        nd = a.ndim
        return pl.BlockSpec(a.shape, lambda s, b: (0,) * nd)

    def modspec(j):
        return pl.BlockSpec((None, None, 1, D), lambda s, b: (mod_row_of_batch(b), j, 0, 0))

    common = [wts["wkv"], wts["wm"], wts["wif"], wts["wift"], wts["bif"], wts["bift"], wts["gk"], wts["bd"]]
    in_specs = [tok(D), modspec(0), modspec(1), full(gpre)] + [full(a) for a in common]
    args = [xin, mod4, mod4, gpre] + common
    out_shape = [jax.ShapeDtypeStruct((B, n, KV_WIDTH), BF16),
                 jax.ShapeDtypeStruct((B, n, KV_WIDTH), BF16),
                 jax.ShapeDtypeStruct((B, n, 2 * MLSTM_WIDTH), BF16),
                 jax.ShapeDtypeStruct((B, n, MLSTM_WIDTH), BF16),
                 jax.ShapeDtypeStruct((B, n, LANES), F32),
                 jax.ShapeDtypeStruct((B, 4 * MLSTM_HEADS, n), F32)]
    out_specs = [tok(KV_WIDTH), tok(KV_WIDTH), tok(2 * MLSTM_WIDTH), tok(MLSTM_WIDTH), tok(LANES),
                 pl.BlockSpec((None, 4 * MLSTM_HEADS, tm), lambda s, b: (b, 0, s))]
    names = ["k", "v", "mqk", "mv", "gc", "gt"]
    if not is_ctx:
        extra = [wts["wq"], wts["wo"], wts["wg"], wts["gq"]]
        in_specs += [full(a) for a in extra]
        args += extra
        cos, sin = tables
        in_specs += [pl.BlockSpec((tm, ATTN_WIDTH), lambda s, b: (s, 0))] * 2
        args += [cos, sin]
        out_shape += [jax.ShapeDtypeStruct((B, n, ATTN_WIDTH), BF16),
                      jax.ShapeDtypeStruct((B, n, MLSTM_WIDTH), BF16),
                      jax.ShapeDtypeStruct((B, n, 2 * D), BF16)]
        out_specs += [tok(ATTN_WIDTH), tok(MLSTM_WIDTH), tok(2 * D)]
        names += ["q", "mo", "gg"]
    outs = pl.pallas_call(
        functools.partial(_inproj_kernel, is_ctx),
        grid=grid, in_specs=in_specs, out_specs=out_specs, out_shape=out_shape,
        compiler_params=_cparams(("parallel", "parallel")),
        name="inproj_ctx" if is_ctx else "inproj",
    )(*args)
    return dict(zip(names, outs))


def _attn_kernel(q_ref, kc_ref, kx_ref, vc_ref, vx_ref, o_ref):
    kc, kx, vc, vx = kc_ref[...], kx_ref[...], vc_ref[...], vx_ref[...]
    tq = q_ref.shape[0]
    lane = lax.broadcasted_iota(I32, (tq, LANES), 1)
    low = lane < ATTN_HEAD_DIM
    n_slab = ATTN_WIDTH // LANES
    for j in range(n_slab):
        slab = q_ref[:, j * LANES:(j + 1) * LANES]
        zs = []
        for part in range(2):
            qh = jnp.where(low if part == 0 else jnp.logical_not(low), slab, jnp.zeros_like(slab))
            sc = _dot_nt(qh, kc)
            sx = _dot_nt(qh, kx)
            m = jnp.maximum(jnp.max(sc, axis=1, keepdims=True), jnp.max(sx, axis=1, keepdims=True))
            pc = jnp.exp(sc - m)
            px = jnp.exp(sx - m)
            l = jnp.sum(pc, axis=1, keepdims=True) + jnp.sum(px, axis=1, keepdims=True)
            z = _dot(pc.astype(BF16), vc) + _dot(px.astype(BF16), vx)
            zs.append(z / l)
        o_ref[:, j * LANES:(j + 1) * LANES] = jnp.where(low, zs[0], zs[1]).astype(BF16)


def _attn_call(q, kc, kx, vc, vx, tq):
    B, S, _ = q.shape
    CL = kc.shape[1]

    def seq(n):
        return pl.BlockSpec((None, n, KV_WIDTH), lambda b, s: (b, 0, 0))

    return pl.pallas_call(
        _attn_kernel,
        grid=(B, S // tq),
        in_specs=[pl.BlockSpec((None, tq, ATTN_WIDTH), lambda b, s: (b, s, 0)),
                  seq(CL), seq(S), seq(CL), seq(S)],
        out_specs=pl.BlockSpec((None, tq, ATTN_WIDTH), lambda b, s: (b, s, 0)),
        out_shape=jax.ShapeDtypeStruct((B, S, ATTN_WIDTH), BF16),
        compiler_params=_cparams(("parallel", "parallel")),
        name="attn",
    )(q, kc, kx, vc, vx)


def _mlstm_kernel(n_chunks_x, n_chunks_c,
                  mqk_x, mv_x, gc_x, gt_x, mqk_c, mv_c, gc_c, gt_c, convw, convb, tl_ref, tu_ref,
                  hf_out, hb_out, qs_x, ks_x, qs_c, ks_c, ct_ref, m_ref):
    L = MLSTM_L
    H = MLSTM_HEADS
    W = MLSTM_WIDTH
    hd = MLSTM_HEAD_DIM
    tl = tl_ref[...]
    tu = tu_ref[...]
    w0, w1, w2, cb = convw[0:1, :], convw[1:2, :], convw[2:3, :], convb[...]
    rows = lax.broadcasted_iota(I32, (L, 2 * W), 0)
    lane2 = lax.broadcasted_iota(I32, (L, 2 * W), 1)
    edge = 2 * SUBLANES

    def conv_chunk(src, nch, qdst, kdst, i):
        start = pl.multiple_of(i * L, L)
        a = src[pl.ds(start, L), :].astype(F32)
        pstart = pl.multiple_of(jnp.maximum(start - edge, 0), edge)
        prev_row = src[pl.ds(pstart, edge), :][edge - 1:edge, :].astype(F32)
        prev_row = jnp.where(i > 0, prev_row, jnp.zeros_like(prev_row))
        nstart = pl.multiple_of(jnp.minimum(start + L, (nch - 1) * L), edge)
        next_row = src[pl.ds(nstart, edge), :][0:1, :].astype(F32)
        next_row = jnp.where(i < nch - 1, next_row, jnp.zeros_like(next_row))
        a_prev = jnp.where(rows == 0, prev_row, pltpu.roll(a, 1, 0))
        a_next = jnp.where(rows == L - 1, next_row, pltpu.roll(a, L - 1, 0))
        y = _silu(w0 * a_prev + w1 * a + w2 * a_next + cb)
        y = jnp.where(lane2 < W, y * (hd ** -0.5), y)
        qdst[pl.ds(start, L), :] = y[:, :W].astype(BF16)
        kdst[pl.ds(start, L), :] = y[:, W:].astype(BF16)

    def conv_c(i, carry):
        conv_chunk(mqk_c, n_chunks_c, qs_c, ks_c, i)
        return carry

    def conv_x(i, carry):
        conv_chunk(mqk_x, n_chunks_x, qs_x, ks_x, i)
        return carry

    lax.fori_loop(0, n_chunks_c, conv_c, 0)
    lax.fori_loop(0, n_chunks_x, conv_x, 0)

    ct_ref[...] = jnp.zeros_like(ct_ref)
    m_ref[...] = jnp.zeros_like(m_ref)

    ri = lax.broadcasted_iota(I32, (L, L), 0)
    ci = lax.broadcasted_iota(I32, (L, L), 1)
    vlane = lax.broadcasted_iota(I32, (L, hd), 1)
    ones_col = jnp.where(vlane == 0, 1.0, 0.0).astype(BF16)

    def chunk_step(qs, ks, mv, gc_ref, gt_ref, start, direction, h_out):
        gc = gc_ref[pl.ds(start, L), :]
        gt = gt_ref[:, pl.ds(start, L)]
        if direction == 0:
            bcol = _dot3_left(tl, gc)
            brow = _dot3_right(gt, tu)
            mask = ci <= ri
        else:
            bcol = _dot3_left(tu, gc)
            brow = _dot3_right(gt, tl)
            mask = ci >= ri
        for hh in range(H):
            c = direction * H + hh
            q_c = qs[pl.ds(start, L), hh * hd:(hh + 1) * hd]
            k_c = ks[pl.ds(start, L), hh * hd:(hh + 1) * hd]
            v_c = mv[pl.ds(start, L), hh * hd:(hh + 1) * hd]
            vaug = jnp.concatenate([v_c, ones_col], axis=1)
            li_col = gc[:, c:c + 1]
            li_row = gt[c:c + 1, :]
            b_col = bcol[:, 2 * H + c:2 * H + c + 1]
            b_row = brow[2 * H + c:2 * H + c + 1, :]
            if direction == 0:
                b_tot = b_row[:, L - 1:L]
            else:
                b_tot = b_row[:, 0:1]
            m_old = m_ref[c:c + 1, 0:1]
            ct = ct_ref[c]
            if h_out is not None:
                dm = jnp.where(mask, b_col - b_row + li_row, NEG_INF)
                inter = b_col + m_old
                m_q = jnp.maximum(inter, jnp.max(dm, axis=1, keepdims=True))
                wmat = jnp.exp(dm - m_q)
                w_inter = jnp.exp(inter - m_q)
                p = (_dot_nt(q_c, k_c) * wmat).astype(BF16)
                r = w_inter * _dot(q_c, ct.astype(BF16)) + _dot(p, vaug)
                den = jnp.maximum(jnp.abs(r[:, hd:hd + 1]), jnp.exp(-m_q))
                h_out[pl.ds(start, L), hh * hd:(hh + 1) * hd] = (r[:, :hd] / den).astype(BF16)
            dec = b_tot - b_col + li_col
            m_new = jnp.maximum(b_tot + m_old, jnp.max(dec, axis=0, keepdims=True))
            w_s = jnp.exp(dec - m_new)
            w_old = jnp.exp(b_tot + m_old - m_new)
            kw = (k_c.astype(F32) * w_s).astype(BF16)
            ct_ref[c] = w_old * ct + _dot_tn(kw, vaug)
            m_ref[c:c + 1, :] = jnp.broadcast_to(m_new, (1, LANES))

    def ctx_body(i, carry):
        chunk_step(qs_c, ks_c, mv_c, gc_c, gt_c, pl.multiple_of(i * L, L), 0, None)
        chunk_step(qs_c, ks_c, mv_c, gc_c, gt_c, pl.multiple_of((n_chunks_c - 1 - i) * L, L), 1, None)
        return carry

    lax.fori_loop(0, n_chunks_c, ctx_body, 0)

    def x_body(i, carry):
        chunk_step(qs_x, ks_x, mv_x, gc_x, gt_x, pl.multiple_of(i * L, L), 0, hf_out)
        chunk_step(qs_x, ks_x, mv_x, gc_x, gt_x, pl.multiple_of((n_chunks_x - 1 - i) * L, L), 1, hb_out)
        return carry

    lax.fori_loop(0, n_chunks_x, x_body, 0)


def _mlstm_call(px, pc, convw, convb):
    B, S, _ = px["mqk"].shape
    CL = pc["mqk"].shape[1]
    L = MLSTM_L
    W = MLSTM_WIDTH
    r = np.arange(L)
    tl = jnp.asarray(r[None, :] <= r[:, None], dtype=BF16)
    tu = jnp.asarray(r[None, :] >= r[:, None], dtype=BF16)

    def seq(n, w):
        return pl.BlockSpec((None, n, w), lambda b: (b, 0, 0))

    def gts(n):
        return pl.BlockSpec((None, 4 * MLSTM_HEADS, n), lambda b: (b, 0, 0))

    def full(a):
        nd = a.ndim
        return pl.BlockSpec(a.shape, lambda b: (0,) * nd)

    return pl.pallas_call(
        functools.partial(_mlstm_kernel, S // L, CL // L),
        grid=(B,),
        in_specs=[seq(S, 2 * W), seq(S, W), seq(S, LANES), gts(S),
                  seq(CL, 2 * W), seq(CL, W), seq(CL, LANES), gts(CL),
                  full(convw), full(convb), full(tl), full(tu)],
        out_specs=[seq(S, W), seq(S, W)],
        out_shape=[jax.ShapeDtypeStruct((B, S, W), BF16)] * 2,
        scratch_shapes=[pltpu.VMEM((S, W), BF16), pltpu.VMEM((S, W), BF16),
                        pltpu.VMEM((CL, W), BF16), pltpu.VMEM((CL, W), BF16),
                        pltpu.VMEM((2 * MLSTM_HEADS, MLSTM_HEAD_DIM, 2 * MLSTM_HEAD_DIM), F32),
                        pltpu.VMEM((2 * MLSTM_HEADS, LANES), F32)],
        compiler_params=_cparams(("parallel",)),
        name="mlstm",
    )(px["mqk"], px["mv"], px["gc"], px["gt"], pc["mqk"], pc["mv"], pc["gc"], pc["gt"], convw, convb, tl, tu)


def _merge_kernel(x_ref, a_ref, hf_ref, hb_ref, mo_ref, gg_ref, g1_ref, sh2_ref, sc2_ref,
                  gml_ref, gpost_ref, gpre_ref, wba_ref, wbm_ref, wout_ref, wrt_ref,
                  x1_out, fin_out, st_out):
    D = D_MODEL
    hs = hf_ref[...].astype(F32) + hb_ref[...].astype(F32)
    gml = gml_ref[...]
    parts = []
    for hh in range(MLSTM_HEADS):
        sl = slice(hh * MLSTM_HEAD_DIM, (hh + 1) * MLSTM_HEAD_DIM)
        parts.append(_rms_rows(hs[:, sl], gml[:, sl]))
    hn = jnp.concatenate(parts, axis=1)
    m_out = (hn * _sigmoid(mo_ref[...].astype(F32))).astype(BF16)
    gg = gg_ref[...].astype(F32)
    y = _sigmoid(gg[:, :D]) * _dot(a_ref[...], wba_ref[...]) + _sigmoid(gg[:, D:]) * _dot(m_out, wbm_ref[...])
    yx = _dot(y.astype(BF16), wout_ref[...])
    x1 = x_ref[...] + g1_ref[...] * _rms_rows(yx, gpost_ref[...])
    x1_out[...] = x1
    fin = _rms_rows(x1, gpre_ref[...]) * (1.0 + sc2_ref[...]) + sh2_ref[...]
    fin_out[...] = fin
    st_out[...] = _sigmoid(_dot_nt(wrt_ref[...], fin.astype(BF16)))


def _merge_call(x, a_out, hf, hb, mo, gg, mod4, wts, tm):
    B, S, D = x.shape
    ns = S // tm

    def tok(w):
        return pl.BlockSpec((None, tm, w), lambda s, b: (b, s, 0))

    def full(a):
        nd = a.ndim
        return pl.BlockSpec(a.shape, lambda s, b: (0,) * nd)

    def modspec(j):
        return pl.BlockSpec((None, None, 1, D), lambda s, b: (b, j, 0, 0))

    consts = [wts["gml"], wts["gpost"], wts["gpre2"], wts["wba"], wts["wbm"], wts["wout"], wts["wrt"]]
    return pl.pallas_call(
        _merge_kernel,
        grid=(ns, B),
        in_specs=[tok(D), tok(ATTN_WIDTH), tok(MLSTM_WIDTH), tok(MLSTM_WIDTH), tok(MLSTM_WIDTH), tok(2 * D),
                  modspec(2), modspec(3), modspec(4)] + [full(a) for a in consts],
        out_specs=[tok(D), tok(D), pl.BlockSpec((N_EXPERTS, tm), lambda s, b: (0, b * ns + s))],
        out_shape=[jax.ShapeDtypeStruct((B, S, D), F32), jax.ShapeDtypeStruct((B, S, D), F32),
                   jax.ShapeDtypeStruct((N_EXPERTS, B * S), F32)],
        compiler_params=_cparams(("parallel", "parallel")),
        name="merge",
    )(x, a_out, hf, hb, mo, gg, mod4, mod4, mod4, *consts)


def _first_argmax(v, idx, big):
    m = jnp.max(v, axis=0, keepdims=True)
    first = jnp.min(jnp.where(v == m, idx, big), axis=0, keepdims=True)
    return m, first


def _route_kernel(st_ref, bias_ref, eidx_out, w_out, cnt_out):
    s = st_ref[...]
    E, tn = s.shape
    sel = s + bias_ref[...]
    gi = lax.broadcasted_iota(I32, (GROUP_SIZE, tn), 0)
    gscores = []
    for g in range(N_GROUPS):
        sg = sel[g * GROUP_SIZE:(g + 1) * GROUP_SIZE, :]
        m1, i1 = _first_argmax(sg, gi, GROUP_SIZE)
        m2 = jnp.max(jnp.where(gi == i1, NEG_INF, sg), axis=0, keepdims=True)
        gscores.append(m1 + m2)
    gs = jnp.concatenate(gscores, axis=0)
    gidx = lax.broadcasted_iota(I32, (N_GROUPS, tn), 0)
    chosen = jnp.zeros((N_GROUPS, tn), dtype=F32)
    for _ in range(TOPK_GROUPS):
        _, ig = _first_argmax(gs, gidx, N_GROUPS)
        hit = gidx == ig
        chosen = jnp.where(hit, 1.0, chosen)
        gs = jnp.where(hit, NEG_INF, gs)
    masked = jnp.concatenate(
        [jnp.where(chosen[g:g + 1, :] > 0.0, sel[g * GROUP_SIZE:(g + 1) * GROUP_SIZE, :], NEG_INF)
         for g in range(N_GROUPS)], axis=0)
    ei = lax.broadcasted_iota(I32, (E, tn), 0)
    idxs, ws = [], []
    member = jnp.zeros((E, tn), dtype=F32)
    for _ in range(TOP_K):
        _, ie = _first_argmax(masked, ei, E)
        hit = ei == ie
        ws.append(jnp.sum(jnp.where(hit, s, 0.0), axis=0, keepdims=True))
        idxs.append(ie)
        member = jnp.where(hit, 1.0, member)
        masked = jnp.where(hit, NEG_INF, masked)
    w = jnp.concatenate(ws, axis=0)
    w = w / jnp.sum(w, axis=0, keepdims=True) * ROUTED_SCALE
    eidx_out[...] = jnp.concatenate(idxs, axis=0)
    w_out[...] = w

    @pl.when(pl.program_id(0) == 0)
    def _():
        cnt_out[...] = jnp.zeros_like(cnt_out)

    cnt_out[...] += jnp.broadcast_to(jnp.sum(member, axis=1, keepdims=True), cnt_out.shape)


def _route_call(st, e_bias_col, tn):
    E, T = st.shape
    return pl.pallas_call(
        _route_kernel,
        grid=(T // tn,),
        in_specs=[pl.BlockSpec((E, tn), lambda i: (0, i)), pl.BlockSpec((E, 1), lambda i: (0, 0))],
        out_specs=[pl.BlockSpec((TOP_K, tn), lambda i: (0, i)), pl.BlockSpec((TOP_K, tn), lambda i: (0, i)),
                   pl.BlockSpec((E, LANES), lambda i: (0, 0))],
        out_shape=[jax.ShapeDtypeStruct((TOP_K, T), I32), jax.ShapeDtypeStruct((TOP_K, T), F32),
                   jax.ShapeDtypeStruct((E, LANES), F32)],
        compiler_params=_cparams(("arbitrary",)),
        name="route",
    )(st, e_bias_col)


def _pos_kernel(eidx_ref, pstart_ref, su_ref, pos_out, run_ref):
    @pl.when(pl.program_id(0) == 0)
    def _():
        run_ref[...] = jnp.zeros_like(run_ref)

    eidx = eidx_ref[...]
    tn = eidx.shape[1]
    E = N_EXPERTS
    ei = lax.broadcasted_iota(I32, (E, tn), 0)
    member = jnp.zeros((E, tn), dtype=F32)
    for k in range(TOP_K):
        member = jnp.where(ei == eidx[k:k + 1, :], 1.0, member)
    prefix = _dot(member.astype(BF16), su_ref[...])
    slot = prefix + (pstart_ref[...] + run_ref[:, 0:1])
    rows = [jnp.sum(jnp.where(ei == eidx[k:k + 1, :], slot, 0.0), axis=0, keepdims=True) for k in range(TOP_K)]
    pos_out[...] = jnp.concatenate(rows, axis=0).astype(I32)
    run_ref[...] += jnp.broadcast_to(jnp.sum(member, axis=1, keepdims=True), run_ref.shape)


def _pos_call(eidx, pstart_col, tn):
    K, T = eidx.shape
    r = np.arange(tn)
    su = jnp.asarray(r[:, None] < r[None, :], dtype=BF16)
    return pl.pallas_call(
        _pos_kernel,
        grid=(T // tn,),
        in_specs=[pl.BlockSpec((K, tn), lambda i: (0, i)), pl.BlockSpec((N_EXPERTS, 1), lambda i: (0, 0)),
                  pl.BlockSpec((tn, tn), lambda i: (0, 0))],
        out_specs=pl.BlockSpec((K, tn), lambda i: (0, i)),
        out_shape=jax.ShapeDtypeStruct((K, T), I32),
        scratch_shapes=[pltpu.VMEM((N_EXPERTS, LANES), F32)],
        compiler_params=_cparams(("arbitrary",)),
        name="pos",
    )(eidx, pstart_col, su)


def _dispatch_kernel(td, pos_ref, fin_hbm, xs_in, xs_out, sem):
    del xs_in
    i = pl.program_id(0)
    n = pl.num_programs(0)
    base = i * td

    def row_copy(t, slot):
        return pltpu.make_async_copy(fin_hbm.at[pl.ds(base + t, 1), :], xs_out.at[pl.ds(slot, 1), :], sem)

    def issue(t, carry):
        for k in range(TOP_K):
            row_copy(t, pos_ref[k, t]).start()
        return carry

    lax.fori_loop(0, td, issue, 0)

    def drain():
        whole = xs_out.at[pl.ds(0, td * TOP_K), :]
        pltpu.make_async_copy(whole, whole, sem).wait()

    @pl.when(i > 0)
    def _():
        drain()

    @pl.when(i == n - 1)
    def _():
        drain()


def _dispatch_call(pos, fin2, xs_zero, td):
    K, T = pos.shape
    return pl.pallas_call(
        functools.partial(_dispatch_kernel, td),
        grid=(T // td,),
        in_specs=[pl.BlockSpec((K, td), lambda i: (0, i), memory_space=pltpu.SMEM),
                  pl.BlockSpec(memory_space=pl.ANY), pl.BlockSpec(memory_space=pl.ANY)],
        out_specs=pl.BlockSpec(memory_space=pl.ANY),
        out_shape=jax.ShapeDtypeStruct(xs_zero.shape, xs_zero.dtype),
        scratch_shapes=[pltpu.SemaphoreType.DMA(())],
        input_output_aliases={2: 0},
        compiler_params=_cparams(("arbitrary",)),
        name="dispatch",
    )(pos, fin2, xs_zero)


def _experts_kernel(be_ref, nu_ref, xs_ref, wg_ref, wu_ref, wd_ref, ys_ref, wgu_s, wd_s):
    i = pl.program_id(0)
    prev = be_ref[jnp.maximum(i - 1, 0)]
    changed = jnp.logical_or(i == 0, be_ref[i] != prev)

    @pl.when(jnp.logical_and(changed, i < nu_ref[0]))
    def _():
        wgu_s[:, :EXPERT_FF] = wg_ref[...].astype(BF16)
        wgu_s[:, EXPERT_FF:] = wu_ref[...].astype(BF16)
        wd_s[...] = wd_ref[...].astype(BF16)

    @pl.when(i < nu_ref[0])
    def _():
        xb = xs_ref[...].astype(BF16)
        gu = _dot(xb, wgu_s[...])
        hid = (_silu(gu[:, :EXPERT_FF]) * gu[:, EXPERT_FF:]).astype(BF16)
        ys_ref[...] = _dot(hid, wd_s[...])


def _experts_call(block_e, n_used, xs, wg, wu, wd):
    P, D = xs.shape
    bm = EXPERT_BM
    nb = P // bm

    def xmap(i, be, nu):
        return (jnp.minimum(i, nu[0] - 1), 0)

    def wmap(i, be, nu):
        return (be[i], 0, 0)

    grid_spec = pltpu.PrefetchScalarGridSpec(
        num_scalar_prefetch=2,
        grid=(nb,),
        in_specs=[pl.BlockSpec((bm, D), xmap),
                  pl.BlockSpec((None, D, EXPERT_FF), wmap),
                  pl.BlockSpec((None, D, EXPERT_FF), wmap),
                  pl.BlockSpec((None, EXPERT_FF, D), wmap)],
        out_specs=pl.BlockSpec((bm, D), xmap),
        scratch_shapes=[pltpu.VMEM((D, 2 * EXPERT_FF), BF16), pltpu.VMEM((EXPERT_FF, D), BF16)],
    )
    return pl.pallas_call(
        _experts_kernel,
        grid_spec=grid_spec,
        out_shape=jax.ShapeDtypeStruct((P, D), F32),
        compiler_params=_cparams(("arbitrary",)),
        name="experts",
    )(block_e, n_used, xs, wg, wu, wd)


def _final_kernel(tf, pos_cur, pos_nxt, ys_hbm, w_ref, fin_ref, x1_ref, g2_ref, gpost_ref,
                  wsg_ref, wsu_ref, wsd_ref, o_ref, gbuf, sem):
    i = pl.program_id(0)
    n = pl.num_programs(0)
    slot = i % 2

    def row_copy(pos_ref, t, k, sl):
        return pltpu.make_async_copy(ys_hbm.at[pl.ds(pos_ref[k, t], 1), :],
                                     gbuf.at[sl, k, pl.ds(t, 1), :], sem.at[sl])

    def issue_from(pos_ref, sl):
        def body(t, carry):
            for k in range(TOP_K):
                row_copy(pos_ref, t, k, sl).start()
            return carry
        lax.fori_loop(0, tf, body, 0)

    @pl.when(i == 0)
    def _():
        issue_from(pos_cur, 0)

    @pl.when(i + 1 < n)
    def _():
        issue_from(pos_nxt, 1 - slot)

    pltpu.make_async_copy(gbuf.at[slot], gbuf.at[slot], sem.at[slot]).wait()

    w = w_ref[...]
    eye = lax.broadcasted_iota(I32, (tf, tf), 0) == lax.broadcasted_iota(I32, (tf, tf), 1)
    acc = jnp.zeros((tf, D_MODEL), dtype=F32)
    for k in range(TOP_K):
        w_col = jnp.sum(jnp.where(eye, w[k:k + 1, :], 0.0), axis=1, keepdims=True)
        acc = acc + w_col * gbuf[slot, k]
    fb = fin_ref[...].astype(BF16)
    hid = (_silu(_dot(fb, wsg_ref[...])) * _dot(fb, wsu_ref[...])).astype(BF16)
    fx = acc + _dot(hid, wsd_ref[...])
    o_ref[...] = x1_ref[...] + g2_ref[...] * _rms_rows(fx, gpost_ref[...])


def _final_call(pos, ys, w, fin2, x1f, mod4, gpost, wsg, wsu, wsd, S, tf):
    K, T = pos.shape
    D = D_MODEL
    n = T // tf
    per_b = S // tf

    def full(a):
        nd = a.ndim
        return pl.BlockSpec(a.shape, lambda i: (0,) * nd)

    return pl.pallas_call(
        functools.partial(_final_kernel, tf),
        grid=(n,),
        in_specs=[pl.BlockSpec((K, tf), lambda i: (0, i), memory_space=pltpu.SMEM),
                  pl.BlockSpec((K, tf), lambda i: (0, jnp.minimum(i + 1, n - 1)), memory_space=pltpu.SMEM),
                  pl.BlockSpec(memory_space=pl.ANY),
                  pl.BlockSpec((K, tf), lambda i: (0, i)),
                  pl.BlockSpec((tf, D), lambda i: (i, 0)),
                  pl.BlockSpec((tf, D), lambda i: (i, 0)),
                  pl.BlockSpec((None, None, 1, D), lambda i: (i // per_b, 5, 0, 0)),
                  full(gpost), full(wsg), full(wsu), full(wsd)],
        out_specs=pl.BlockSpec((tf, D), lambda i: (i, 0)),
        out_shape=jax.ShapeDtypeStruct((T, D), F32),
        scratch_shapes=[pltpu.VMEM((2, K, tf, D), F32), pltpu.SemaphoreType.DMA((2,))],
        compiler_params=_cparams(("arbitrary",)),
        name="final",
    )(pos, pos, ys, w, fin2, x1f, mod4, gpost, wsg, wsu, wsd)


def _rope_tables(S):
    nf = ATTN_HEAD_DIM // 4
    t = jnp.arange(S, dtype=jnp.int32)
    pos = jnp.stack([t // GRID_W, t % GRID_W], axis=-1).astype(F32)
    inv_freq = ROPE_THETA ** (-jnp.arange(nf, dtype=F32) / nf)
    ang = pos[:, :, None] * inv_freq
    cos, sin = jnp.cos(ang), jnp.sin(ang)
    cos_h = jnp.concatenate([cos, cos], axis=-1).reshape(S, ATTN_HEAD_DIM)
    sin_h = jnp.concatenate([-sin, sin], axis=-1).reshape(S, ATTN_HEAD_DIM)
    return jnp.tile(cos_h, (1, ATTN_HEADS)), jnp.tile(sin_h, (1, ATTN_HEADS))


def _pick_tile(n, pref):
    t = min(n, pref)
    while n % t:
        t //= 2
    return t


def _layer(x, ctx, cc, l, p):
    B, S, D = x.shape
    CL = ctx.shape[1]
    T = B * S
    NBp = cc.shape[0]

    mod = _ada_call(cc, p["w_ada"][l], p["b_ada"][l][None, :])
    mod4 = mod.reshape(NBp, N_MOD, 1, D)

    w_in = p["w_in"][l]
    o = np.cumsum((0, ATTN_WIDTH, KV_WIDTH, KV_WIDTH, MLSTM_WIDTH, MLSTM_WIDTH, MLSTM_WIDTH, MLSTM_WIDTH,
                   2 * MLSTM_HEADS, 2 * MLSTM_HEADS, D, D))
    half = ATTN_HEADS // 2
    head_order = np.stack([np.arange(half), np.arange(half) + half], axis=1).reshape(-1)
    qperm = (head_order[:, None] * ATTN_HEAD_DIM + np.arange(ATTN_HEAD_DIM)[None, :]).reshape(-1)
    wq = w_in[:, o[0]:o[1]][:, qperm].astype(BF16)
    wkv = w_in[:, o[1]:o[3]].astype(BF16)
    wm = w_in[:, o[3]:o[6]].astype(BF16)
    wo = w_in[:, o[6]:o[7]].astype(BF16)
    w_gates = w_in[:, o[7]:o[9]]
    wif = jnp.pad(w_gates, ((0, 0), (0, LANES - 4 * MLSTM_HEADS))).astype(BF16)
    wift = w_gates.T.astype(BF16)
    wg = w_in[:, o[9]:o[11]].astype(BF16)
    b_gates = jnp.concatenate([p["b_igate"][l], p["b_fgate"][l]]).astype(F32)
    bif = jnp.pad(b_gates, (0, LANES - 4 * MLSTM_HEADS))[None, :]
    bift = b_gates[:, None]
    hidx = np.arange(ATTN_WIDTH) // ATTN_HEAD_DIM
    bd = jnp.asarray(hidx[:, None] == hidx[None, :], dtype=BF16)
    wts = dict(wq=wq, wkv=wkv, wm=wm, wo=wo, wg=wg, wif=wif, wift=wift, bif=bif, bift=bift,
               gq=jnp.tile(p["g_q"][l], ATTN_HEADS)[None, :], gk=jnp.tile(p["g_k"][l], ATTN_KV_HEADS)[None, :],
               bd=bd)
    gpre = p["g_pre_mix"][l][None, :]

    tm = _pick_tile(S, 256)
    px = _inproj_call(False, x, mod4, lambda b: b, gpre, wts, _rope_tables(S), tm)
    pc = _inproj_call(True, ctx, mod4, lambda b: B, gpre, wts, None, _pick_tile(CL, 256))

    a_out = _attn_call(px["q"], pc["k"], px["k"], pc["v"], px["v"], _pick_tile(S, 256))
    hf, hb = _mlstm_call(px, pc, p["conv_w"][l], p["conv_b"][l][None, :])

    mw = dict(gml=p["g_mlstm"][l][None, :], gpost=p["g_post_mix"][l][None, :], gpre2=p["g_pre_ffn"][l][None, :],
              wba=p["w_br_attn"][l][qperm, :].astype(BF16), wbm=p["w_br_mlstm"][l].astype(BF16),
              wout=p["w_out"][l].astype(BF16), wrt=p["w_router"][l].T.astype(BF16))
    x1, fin, st = _merge_call(x, a_out, hf, hb, px["mo"], px["gg"], mod4, mw, tm)

    tn = _pick_tile(T, 512)
    eidx, w, cnt = _route_call(st, p["e_bias"][l].astype(F32)[:, None], tn)
    counts = cnt[:, 0].astype(I32)
    bm = EXPERT_BM
    padded = (counts + bm - 1) // bm * bm
    pend = jnp.cumsum(padded)
    pstart = pend - padded
    n_blocks = (T * TOP_K + N_EXPERTS * (bm - 1) + bm - 1) // bm
    block_e = jnp.minimum(jnp.searchsorted(pend, jnp.arange(n_blocks, dtype=I32) * bm, side="right"),
                          N_EXPERTS - 1).astype(I32)
    n_used = (pend[-1:] // bm).astype(I32)
    pos = _pos_call(eidx, pstart.astype(F32)[:, None], tn)

    fin2 = fin.reshape(T, D)
    xs = _dispatch_call(pos, fin2, jnp.zeros((n_blocks * bm, D), F32), _pick_tile(T, 512))
    ys = _experts_call(block_e, n_used, xs, p["w_exp_gate"][l], p["w_exp_up"][l], p["w_exp_down"][l])
    out = _final_call(pos, ys, w, fin2, x1.reshape(T, D), mod4, p["g_post_ffn"][l][None, :],
                      p["w_sh_gate"][l].astype(BF16), p["w_sh_up"][l].astype(BF16),
                      p["w_sh_down"][l].astype(BF16), S, _pick_tile(S, 128))
    return out.reshape(B, S, D)


def kernel(x, c, ctx, c_ctx, w_ada, b_ada, g_pre_mix, g_post_mix, g_pre_ffn, g_post_ffn, w_in, g_q, g_k, conv_w, conv_b, b_igate, b_fgate, g_mlstm, w_br_attn, w_br_mlstm, w_out, w_router, e_bias, w_exp_gate, w_exp_up, w_exp_down, w_sh_gate, w_sh_up, w_sh_down):
    depth = w_ada.shape[0]
    assert depth == 1, "context-token updates (needed only between layers) are not implemented"
    B = x.shape[0]
    nbp = (B + 1 + SUBLANES - 1) // SUBLANES * SUBLANES
    cc = jnp.concatenate([c, c_ctx[None, :], jnp.zeros((nbp - B - 1, c.shape[1]), c.dtype)], axis=0)
    p = dict(w_ada=w_ada, b_ada=b_ada, g_pre_mix=g_pre_mix, g_post_mix=g_post_mix, g_pre_ffn=g_pre_ffn,
             g_post_ffn=g_post_ffn, w_in=w_in, g_q=g_q, g_k=g_k, conv_w=conv_w, conv_b=conv_b, b_igate=b_igate,
             b_fgate=b_fgate, g_mlstm=g_mlstm, w_br_attn=w_br_attn, w_br_mlstm=w_br_mlstm, w_out=w_out,
             w_router=w_router, e_bias=e_bias, w_exp_gate=w_exp_gate, w_exp_up=w_exp_up, w_exp_down=w_exp_down,
             w_sh_gate=w_sh_gate, w_sh_up=w_sh_up, w_sh_down=w_sh_down)
    return _layer(x, ctx, cc, 0, p)
```

```python
import functools

import numpy as np
import jax
import jax.numpy as jnp
from jax import lax
from jax.experimental import pallas as pl
from jax.experimental.pallas import tpu as pltpu

F32 = jnp.float32
BF16 = jnp.bfloat16
I32 = jnp.int32

D_MODEL = 1024
GRID_W = 64
EPS = 1e-6
N_MOD = 6
ATTN_HEADS = 8
ATTN_KV_HEADS = 2
ATTN_HEAD_DIM = 64
ATTN_WIDTH = ATTN_HEADS * ATTN_HEAD_DIM
KV_WIDTH = ATTN_KV_HEADS * ATTN_HEAD_DIM
ROPE_THETA = 10000.0
MLSTM_HEADS = 4
MLSTM_HEAD_DIM = 128
MLSTM_WIDTH = MLSTM_HEADS * MLSTM_HEAD_DIM
CONV_W = 3
N_EXPERTS = 256
TOP_K = 8
N_GROUPS = 8
TOPK_GROUPS = 4
GROUP_SIZE = N_EXPERTS // N_GROUPS
EXPERT_FF = 256
SHARED_FF = 256
ROUTED_SCALE = 2.5

LANES = 128
SUBLANES = 8
VMEM_LIMIT = 56 * 1024 * 1024

MLSTM_L = 128
EXPERT_BM = 256
NEG_INF = float("-inf")


def _cparams(sem):
    return pltpu.CompilerParams(dimension_semantics=sem, vmem_limit_bytes=VMEM_LIMIT)


def _split3(a):
    hi = a.astype(BF16)
    r1 = a - hi.astype(F32)
    mid = r1.astype(BF16)
    lo = (r1 - mid.astype(F32)).astype(BF16)
    return hi, mid, lo


def _dot(a, b):
    return jnp.dot(a, b, preferred_element_type=F32)


def _dot_nt(a, b):
    return lax.dot_general(a, b, (((1,), (1,)), ((), ())), preferred_element_type=F32)


def _dot_tn(a, b):
    return lax.dot_general(a, b, (((0,), (0,)), ((), ())), preferred_element_type=F32)


def _dot3_right(a_f32, t_bf16):
    hi, mid, lo = _split3(a_f32)
    return _dot(hi, t_bf16) + _dot(mid, t_bf16) + _dot(lo, t_bf16)


def _dot3_left(t_bf16, a_f32):
    hi, mid, lo = _split3(a_f32)
    return _dot(t_bf16, hi) + _dot(t_bf16, mid) + _dot(t_bf16, lo)


def _sigmoid(x):
    return 1.0 / (1.0 + jnp.exp(-x))


def _silu(x):
    return x * _sigmoid(x)


def _log_sigmoid(x):
    return jnp.minimum(x, 0.0) - jnp.log(1.0 + jnp.exp(-jnp.abs(x)))


def _rms_rows(x, g):
    return x * lax.rsqrt(jnp.mean(x * x, axis=-1, keepdims=True) + EPS) * g


def _ada_kernel(c_ref, w_ref, b_ref, o_ref):
    a = _silu(c_ref[...]).astype(BF16)
    o_ref[...] = _dot(a, w_ref[...].astype(BF16)) + b_ref[...]


def _ada_call(cc, w_ada, b_ada):
    nb, d = cc.shape
    n = w_ada.shape[1]
    tn = 1536
    return pl.pallas_call(
        _ada_kernel,
        grid=(n // tn,),
        in_specs=[pl.BlockSpec((nb, d), lambda j: (0, 0)),
                  pl.BlockSpec((d, tn), lambda j: (0, j)),
                  pl.BlockSpec((1, tn), lambda j: (0, j))],
        out_specs=pl.BlockSpec((nb, tn), lambda j: (0, j)),
        out_shape=jax.ShapeDtypeStruct((nb, n), F32),
        compiler_params=_cparams(("parallel",)),
        name="ada",
    )(cc, w_ada, b_ada)


def _head_norm(x, g, bd):
    ss = _dot3_right(x * x, bd)
    return x * lax.rsqrt(ss * (1.0 / ATTN_HEAD_DIM) + EPS) * g


def _rope(x, cos, sin_signed):
    w = x.shape[1]
    q = ATTN_HEAD_DIM // 4
    lane = lax.broadcasted_iota(I32, x.shape, 1)
    first = (lane % (2 * q)) < q
    partner = jnp.where(first, pltpu.roll(x, w - q, 1), pltpu.roll(x, q, 1))
    return x * cos + partner * sin_signed


def _inproj_kernel(is_ctx, *refs):
    if is_ctx:
        (x_ref, sh_ref, sc_ref, gpre_ref, wkv_ref, wm_ref, wif_ref, wift_ref, bif_ref, bift_ref,
         gk_ref, bd_ref, k_out, v_out, mqk_out, mv_out, gc_out, gt_out) = refs
    else:
        (x_ref, sh_ref, sc_ref, gpre_ref, wkv_ref, wm_ref, wif_ref, wift_ref, bif_ref, bift_ref,
         gk_ref, bd_ref, wq_ref, wo_ref, wg_ref, gq_ref, cos_ref, sin_ref,
         k_out, v_out, mqk_out, mv_out, gc_out, gt_out, q_out, mo_out, gg_out) = refs
    x = x_ref[...]
    h = _rms_rows(x, gpre_ref[...])
    h = h * (1.0 + sc_ref[...]) + sh_ref[...]
    hb = h.astype(BF16)

    kv = _dot(hb, wkv_ref[...])
    k = _head_norm(kv[:, :KV_WIDTH], gk_ref[...], bd_ref[:KV_WIDTH, :KV_WIDTH])
    if not is_ctx:
        k = _rope(k, cos_ref[:, :KV_WIDTH], sin_ref[:, :KV_WIDTH])
    k_out[...] = k.astype(BF16)
    v_out[...] = kv[:, KV_WIDTH:].astype(BF16)

    m = _dot(hb, wm_ref[...])
    mqk_out[...] = m[:, :2 * MLSTM_WIDTH].astype(BF16)
    mv_out[...] = m[:, 2 * MLSTM_WIDTH:].astype(BF16)

    gc = _dot(hb, wif_ref[...]) + bif_ref[...]
    lane = lax.broadcasted_iota(I32, gc.shape, 1)
    gc_out[...] = jnp.where(lane < 2 * MLSTM_HEADS, gc, _log_sigmoid(gc))
    gt = _dot_nt(wift_ref[...], hb) + bift_ref[...]
    row = lax.broadcasted_iota(I32, gt.shape, 0)
    gt_out[...] = jnp.where(row < 2 * MLSTM_HEADS, gt, _log_sigmoid(gt))

    if not is_ctx:
        q = _dot(hb, wq_ref[...])
        q = _head_norm(q, gq_ref[...], bd_ref[...])
        q = _rope(q, cos_ref[...], sin_ref[...]) * (ATTN_HEAD_DIM ** -0.5)
        q_out[...] = q.astype(BF16)
        mo_out[...] = _dot(hb, wo_ref[...]).astype(BF16)
        gg_out[...] = _dot(hb, wg_ref[...]).astype(BF16)


def _inproj_call(is_ctx, xin, mod4, mod_row_of_batch, gpre, wts, tables, tm):
    B, n, D = xin.shape
    ns = n // tm
    grid = (ns, B)

    def tok(w):
        return pl.BlockSpec((None, tm, w), lambda s, b: (b, s, 0))

    def full(a):
        nd = a.ndim
        return pl.BlockSpec(a.shape, lambda s, b: (0,) * nd)

    def modspec(j):
        return pl.BlockSpec((None, None, 1, D), lambda s, b: (mod_row_of_batch(b), j, 0, 0))

    common = [wts["wkv"], wts["wm"], wts["wif"], wts["wift"], wts["bif"], wts["bift"], wts["gk"], wts["bd"]]
    in_specs = [tok(D), modspec(0), modspec(1), full(gpre)] + [full(a) for a in common]
    args = [xin, mod4, mod4, gpre] + common
    out_shape = [jax.ShapeDtypeStruct((B, n, KV_WIDTH), BF16),
                 jax.ShapeDtypeStruct((B, n, KV_WIDTH), BF16),
                 jax.ShapeDtypeStruct((B, n, 2 * MLSTM_WIDTH), BF16),
                 jax.ShapeDtypeStruct((B, n, MLSTM_WIDTH), BF16),
                 jax.ShapeDtypeStruct((B, n, LANES), F32),
                 jax.ShapeDtypeStruct((B, 4 * MLSTM_HEADS, n), F32)]
    out_specs = [tok(KV_WIDTH), tok(KV_WIDTH), tok(2 * MLSTM_WIDTH), tok(MLSTM_WIDTH), tok(LANES),
                 pl.BlockSpec((None, 4 * MLSTM_HEADS, tm), lambda s, b: (b, 0, s))]
    names = ["k", "v", "mqk", "mv", "gc", "gt"]
    if not is_ctx:
        extra = [wts["wq"], wts["wo"], wts["wg"], wts["gq"]]
        in_specs += [full(a) for a in extra]
        args += extra
        cos, sin = tables
        in_specs += [pl.BlockSpec((tm, ATTN_WIDTH), lambda s, b: (s, 0))] * 2
        args += [cos, sin]
        out_shape += [jax.ShapeDtypeStruct((B, n, ATTN_WIDTH), BF16),
                      jax.ShapeDtypeStruct((B, n, MLSTM_WIDTH), BF16),
                      jax.ShapeDtypeStruct((B, n, 2 * D), BF16)]
        out_specs += [tok(ATTN_WIDTH), tok(MLSTM_WIDTH), tok(2 * D)]
        names += ["q", "mo", "gg"]
    outs = pl.pallas_call(
        functools.partial(_inproj_kernel, is_ctx),
        grid=grid, in_specs=in_specs, out_specs=out_specs, out_shape=out_shape,
        compiler_params=_cparams(("parallel", "parallel")),
        name="inproj_ctx" if is_ctx else "inproj",
    )(*args)
    return dict(zip(names, outs))


def _attn_kernel(q_ref, kc_ref, kx_ref, vc_ref, vx_ref, o_ref):
    kc, kx, vc, vx = kc_ref[...], kx_ref[...], vc_ref[...], vx_ref[...]
    tq = q_ref.shape[0]
    lane = lax.broadcasted_iota(I32, (tq, LANES), 1)
    low = lane < ATTN_HEAD_DIM
    n_slab = ATTN_WIDTH // LANES
    for j in range(n_slab):
        slab = q_ref[:, j * LANES:(j + 1) * LANES]
        zs = []
        for part in range(2):
            qh = jnp.where(low if part == 0 else jnp.logical_not(low), slab, jnp.zeros_like(slab))
            sc = _dot_nt(qh, kc)
            sx = _dot_nt(qh, kx)
            m = jnp.maximum(jnp.max(sc, axis=1, keepdims=True), jnp.max(sx, axis=1, keepdims=True))
            pc = jnp.exp(sc - m)
            px = jnp.exp(sx - m)
            l = jnp.sum(pc, axis=1, keepdims=True) + jnp.sum(px, axis=1, keepdims=True)
            z = _dot(pc.astype(BF16), vc) + _dot(px.astype(BF16), vx)
            zs.append(z / l)
        o_ref[:, j * LANES:(j + 1) * LANES] = jnp.where(low, zs[0], zs[1]).astype(BF16)


def _attn_call(q, kc, kx, vc, vx, tq):
    B, S, _ = q.shape
    CL = kc.shape[1]

    def seq(n):
        return pl.BlockSpec((None, n, KV_WIDTH), lambda b, s: (b, 0, 0))

    return pl.pallas_call(
        _attn_kernel,
        grid=(B, S // tq),
        in_specs=[pl.BlockSpec((None, tq, ATTN_WIDTH), lambda b, s: (b, s, 0)),
                  seq(CL), seq(S), seq(CL), seq(S)],
        out_specs=pl.BlockSpec((None, tq, ATTN_WIDTH), lambda b, s: (b, s, 0)),
        out_shape=jax.ShapeDtypeStruct((B, S, ATTN_WIDTH), BF16),
        compiler_params=_cparams(("parallel", "parallel")),
        name="attn",
    )(q, kc, kx, vc, vx)


def _mlstm_kernel(n_chunks_x, n_chunks_c,
                  mqk_x, mv_x, gc_x, gt_x, mqk_c, mv_c, gc_c, gt_c, convw, convb, tl_ref, tu_ref,
                  hf_out, hb_out, qs_x, ks_x, qs_c, ks_c, ct_ref, m_ref):
    L = MLSTM_L
    H = MLSTM_HEADS
    W = MLSTM_WIDTH
    hd = MLSTM_HEAD_DIM
    tl = tl_ref[...]
    tu = tu_ref[...]
    w0, w1, w2, cb = convw[0:1, :], convw[1:2, :], convw[2:3, :], convb[...]
    rows = lax.broadcasted_iota(I32, (L, 2 * W), 0)
    lane2 = lax.broadcasted_iota(I32, (L, 2 * W), 1)
    edge = 2 * SUBLANES

    def conv_chunk(src, nch, qdst, kdst, i):
        start = pl.multiple_of(i * L, L)
        a = src[pl.ds(start, L), :].astype(F32)
        pstart = pl.multiple_of(jnp.maximum(start - edge, 0), edge)
        prev_row = src[pl.ds(pstart, edge), :][edge - 1:edge, :].astype(F32)
        prev_row = jnp.where(i > 0, prev_row, jnp.zeros_like(prev_row))
        nstart = pl.multiple_of(jnp.minimum(start + L, (nch - 1) * L), edge)
        next_row = src[pl.ds(nstart, edge), :][0:1, :].astype(F32)
        next_row = jnp.where(i < nch - 1, next_row, jnp.zeros_like(next_row))
        a_prev = jnp.where(rows == 0, prev_row, pltpu.roll(a, 1, 0))
        a_next = jnp.where(rows == L - 1, next_row, pltpu.roll(a, L - 1, 0))
        y = _silu(w0 * a_prev + w1 * a + w2 * a_next + cb)
        y = jnp.where(lane2 < W, y * (hd ** -0.5), y)
        qdst[pl.ds(start, L), :] = y[:, :W].astype(BF16)
        kdst[pl.ds(start, L), :] = y[:, W:].astype(BF16)

    def conv_c(i, carry):
        conv_chunk(mqk_c, n_chunks_c, qs_c, ks_c, i)
        return carry

    def conv_x(i, carry):
        conv_chunk(mqk_x, n_chunks_x, qs_x, ks_x, i)
        return carry

    lax.fori_loop(0, n_chunks_c, conv_c, 0)
    lax.fori_loop(0, n_chunks_x, conv_x, 0)

    ct_ref[...] = jnp.zeros_like(ct_ref)
    m_ref[...] = jnp.zeros_like(m_ref)

    ri = lax.broadcasted_iota(I32, (L, L), 0)
    ci = lax.broadcasted_iota(I32, (L, L), 1)
    vlane = lax.broadcasted_iota(I32, (L, hd), 1)
    ones_col = jnp.where(vlane == 0, 1.0, 0.0).astype(BF16)

    def chunk_step(qs, ks, mv, gc_ref, gt_ref, start, direction, h_out):
        gc = gc_ref[pl.ds(start, L), :]
        gt = gt_ref[:, pl.ds(start, L)]
        if direction == 0:
            bcol = _dot3_left(tl, gc)
            brow = _dot3_right(gt, tu)
            mask = ci <= ri
        else:
            bcol = _dot3_left(tu, gc)
            brow = _dot3_right(gt, tl)
            mask = ci >= ri
        for hh in range(H):
            c = direction * H + hh
            q_c = qs[pl.ds(start, L), hh * hd:(hh + 1) * hd]
            k_c = ks[pl.ds(start, L), hh * hd:(hh + 1) * hd]
            v_c = mv[pl.ds(start, L), hh * hd:(hh + 1) * hd]
            vaug = jnp.concatenate([v_c, ones_col], axis=1)
            li_col = gc[:, c:c + 1]
            li_row = gt[c:c + 1, :]
            b_col = bcol[:, 2 * H + c:2 * H + c + 1]
            b_row = brow[2 * H + c:2 * H + c + 1, :]
            if direction == 0:
                b_tot = b_row[:, L - 1:L]
            else:
                b_tot = b_row[:, 0:1]
            m_old = m_ref[c:c + 1, 0:1]
            ct = ct_ref[c]
            if h_out is not None:
                dm = jnp.where(mask, b_col - b_row + li_row, NEG_INF)
                inter = b_col + m_old
                m_q = jnp.maximum(inter, jnp.max(dm, axis=1, keepdims=True))
                wmat = jnp.exp(dm - m_q)
                w_inter = jnp.exp(inter - m_q)
                p = (_dot_nt(q_c, k_c) * wmat).astype(BF16)
                r = w_inter * _dot(q_c, ct.astype(BF16)) + _dot(p, vaug)
                den = jnp.maximum(jnp.abs(r[:, hd:hd + 1]), jnp.exp(-m_q))
                h_out[pl.ds(start, L), hh * hd:(hh + 1) * hd] = (r[:, :hd] / den).astype(BF16)
            dec = b_tot - b_col + li_col
            m_new = jnp.maximum(b_tot + m_old, jnp.max(dec, axis=0, keepdims=True))
            w_s = jnp.exp(dec - m_new)
            w_old = jnp.exp(b_tot + m_old - m_new)
            kw = (k_c.astype(F32) * w_s).astype(BF16)
            ct_ref[c] = w_old * ct + _dot_tn(kw, vaug)
            m_ref[c:c + 1, :] = jnp.broadcast_to(m_new, (1, LANES))

    def ctx_body(i, carry):
        chunk_step(qs_c, ks_c, mv_c, gc_c, gt_c, pl.multiple_of(i * L, L), 0, None)
        chunk_step(qs_c, ks_c, mv_c, gc_c, gt_c, pl.multiple_of((n_chunks_c - 1 - i) * L, L), 1, None)
        return carry

    lax.fori_loop(0, n_chunks_c, ctx_body, 0)

    def x_body(i, carry):
        chunk_step(qs_x, ks_x, mv_x, gc_x, gt_x, pl.multiple_of(i * L, L), 0, hf_out)
        chunk_step(qs_x, ks_x, mv_x, gc_x, gt_x, pl.multiple_of((n_chunks_x - 1 - i) * L, L), 1, hb_out)
        return carry

    lax.fori_loop(0, n_chunks_x, x_body, 0)


def _mlstm_call(px, pc, convw, convb):
    B, S, _ = px["mqk"].shape
    CL = pc["mqk"].shape[1]
    L = MLSTM_L
    W = MLSTM_WIDTH
    r = np.arange(L)
    tl = jnp.asarray(r[None, :] <= r[:, None], dtype=BF16)
    tu = jnp.asarray(r[None, :] >= r[:, None], dtype=BF16)

    def seq(n, w):
        return pl.BlockSpec((None, n, w), lambda b: (b, 0, 0))

    def gts(n):
        return pl.BlockSpec((None, 4 * MLSTM_HEADS, n), lambda b: (b, 0, 0))

    def full(a):
        nd = a.ndim
        return pl.BlockSpec(a.shape, lambda b: (0,) * nd)

    return pl.pallas_call(
        functools.partial(_mlstm_kernel, S // L, CL // L),
        grid=(B,),
        in_specs=[seq(S, 2 * W), seq(S, W), seq(S, LANES), gts(S),
                  seq(CL, 2 * W), seq(CL, W), seq(CL, LANES), gts(CL),
                  full(convw), full(convb), full(tl), full(tu)],
        out_specs=[seq(S, W), seq(S, W)],
        out_shape=[jax.ShapeDtypeStruct((B, S, W), BF16)] * 2,
        scratch_shapes=[pltpu.VMEM((S, W), BF16), pltpu.VMEM((S, W), BF16),
                        pltpu.VMEM((CL, W), BF16), pltpu.VMEM((CL, W), BF16),
                        pltpu.VMEM((2 * MLSTM_HEADS, MLSTM_HEAD_DIM, 2 * MLSTM_HEAD_DIM), F32),
                        pltpu.VMEM((2 * MLSTM_HEADS, LANES), F32)],
        compiler_params=_cparams(("parallel",)),
        name="mlstm",
    )(px["mqk"], px["mv"], px["gc"], px["gt"], pc["mqk"], pc["mv"], pc["gc"], pc["gt"], convw, convb, tl, tu)


def _merge_kernel(x_ref, a_ref, hf_ref, hb_ref, mo_ref, gg_ref, g1_ref, sh2_ref, sc2_ref,
                  gml_ref, gpost_ref, gpre_ref, wba_ref, wbm_ref, wout_ref, wrt_ref,
                  x1_out, fin_out, st_out):
    D = D_MODEL
    hs = hf_ref[...].astype(F32) + hb_ref[...].astype(F32)
    gml = gml_ref[...]
    parts = []
    for hh in range(MLSTM_HEADS):
        sl = slice(hh * MLSTM_HEAD_DIM, (hh + 1) * MLSTM_HEAD_DIM)
        parts.append(_rms_rows(hs[:, sl], gml[:, sl]))
    hn = jnp.concatenate(parts, axis=1)
    m_out = (hn * _sigmoid(mo_ref[...].astype(F32))).astype(BF16)
    gg = gg_ref[...].astype(F32)
    y = _sigmoid(gg[:, :D]) * _dot(a_ref[...], wba_ref[...]) + _sigmoid(gg[:, D:]) * _dot(m_out, wbm_ref[...])
    yx = _dot(y.astype(BF16), wout_ref[...])
    x1 = x_ref[...] + g1_ref[...] * _rms_rows(yx, gpost_ref[...])
    x1_out[...] = x1
    fin = _rms_rows(x1, gpre_ref[...]) * (1.0 + sc2_ref[...]) + sh2_ref[...]
    fin_out[...] = fin
    st_out[...] = _sigmoid(_dot_nt(wrt_ref[...], fin.astype(BF16)))


def _merge_call(x, a_out, hf, hb, mo, gg, mod4, wts, tm):
    B, S, D = x.shape
    ns = S // tm

    def tok(w):
        return pl.BlockSpec((None, tm, w), lambda s, b: (b, s, 0))

    def full(a):
        nd = a.ndim
        return pl.BlockSpec(a.shape, lambda s, b: (0,) * nd)

    def modspec(j):
        return pl.BlockSpec((None, None, 1, D), lambda s, b: (b, j, 0, 0))

    consts = [wts["gml"], wts["gpost"], wts["gpre2"], wts["wba"], wts["wbm"], wts["wout"], wts["wrt"]]
    return pl.pallas_call(
        _merge_kernel,
        grid=(ns, B),
        in_specs=[tok(D), tok(ATTN_WIDTH), tok(MLSTM_WIDTH), tok(MLSTM_WIDTH), tok(MLSTM_WIDTH), tok(2 * D),
                  modspec(2), modspec(3), modspec(4)] + [full(a) for a in consts],
        out_specs=[tok(D), tok(D), pl.BlockSpec((N_EXPERTS, tm), lambda s, b: (0, b * ns + s))],
        out_shape=[jax.ShapeDtypeStruct((B, S, D), F32), jax.ShapeDtypeStruct((B, S, D), F32),
                   jax.ShapeDtypeStruct((N_EXPERTS, B * S), F32)],
        compiler_params=_cparams(("parallel", "parallel")),
        name="merge",
    )(x, a_out, hf, hb, mo, gg, mod4, mod4, mod4, *consts)


def _first_argmax(v, idx, big):
    m = jnp.max(v, axis=0, keepdims=True)
    first = jnp.min(jnp.where(v == m, idx, big), axis=0, keepdims=True)
    return m, first


def _route_kernel(st_ref, bias_ref, eidx_out, w_out, cnt_out):
    s = st_ref[...]
    E, tn = s.shape
    sel = s + bias_ref[...]
    gi = lax.broadcasted_iota(I32, (GROUP_SIZE, tn), 0)
    gscores = []
    for g in range(N_GROUPS):
        sg = sel[g * GROUP_SIZE:(g + 1) * GROUP_SIZE, :]
        m1, i1 = _first_argmax(sg, gi, GROUP_SIZE)
        m2 = jnp.max(jnp.where(gi == i1, NEG_INF, sg), axis=0, keepdims=True)
        gscores.append(m1 + m2)
    gs = jnp.concatenate(gscores, axis=0)
    gidx = lax.broadcasted_iota(I32, (N_GROUPS, tn), 0)
    chosen = jnp.zeros((N_GROUPS, tn), dtype=F32)
    for _ in range(TOPK_GROUPS):
        _, ig = _first_argmax(gs, gidx, N_GROUPS)
        hit = gidx == ig
        chosen = jnp.where(hit, 1.0, chosen)
        gs = jnp.where(hit, NEG_INF, gs)
    masked = jnp.concatenate(
        [jnp.where(chosen[g:g + 1, :] > 0.0, sel[g * GROUP_SIZE:(g + 1) * GROUP_SIZE, :], NEG_INF)
         for g in range(N_GROUPS)], axis=0)
    ei = lax.broadcasted_iota(I32, (E, tn), 0)
    idxs, ws = [], []
    member = jnp.zeros((E, tn), dtype=F32)
    for _ in range(TOP_K):
        _, ie = _first_argmax(masked, ei, E)
        hit = ei == ie
        ws.append(jnp.sum(jnp.where(hit, s, 0.0), axis=0, keepdims=True))
        idxs.append(ie)
        member = jnp.where(hit, 1.0, member)
        masked = jnp.where(hit, NEG_INF, masked)
    w = jnp.concatenate(ws, axis=0)
    w = w / jnp.sum(w, axis=0, keepdims=True) * ROUTED_SCALE
    eidx_out[...] = jnp.concatenate(idxs, axis=0)
    w_out[...] = w

    @pl.when(pl.program_id(0) == 0)
    def _():
        cnt_out[...] = jnp.zeros_like(cnt_out)

    cnt_out[...] += jnp.broadcast_to(jnp.sum(member, axis=1, keepdims=True), cnt_out.shape)


def _route_call(st, e_bias_col, tn):
    E, T = st.shape
    return pl.pallas_call(
        _route_kernel,
        grid=(T // tn,),
        in_specs=[pl.BlockSpec((E, tn), lambda i: (0, i)), pl.BlockSpec((E, 1), lambda i: (0, 0))],
        out_specs=[pl.BlockSpec((TOP_K, tn), lambda i: (0, i)), pl.BlockSpec((TOP_K, tn), lambda i: (0, i)),
                   pl.BlockSpec((E, LANES), lambda i: (0, 0))],
        out_shape=[jax.ShapeDtypeStruct((TOP_K, T), I32), jax.ShapeDtypeStruct((TOP_K, T), F32),
                   jax.ShapeDtypeStruct((E, LANES), F32)],
        compiler_params=_cparams(("arbitrary",)),
        name="route",
    )(st, e_bias_col)


def _pos_kernel(eidx_ref, pstart_ref, su_ref, pos_out, run_ref):
    @pl.when(pl.program_id(0) == 0)
    def _():
        run_ref[...] = jnp.zeros_like(run_ref)

    eidx = eidx_ref[...]
    tn = eidx.shape[1]
    E = N_EXPERTS
    ei = lax.broadcasted_iota(I32, (E, tn), 0)
    member = jnp.zeros((E, tn), dtype=F32)
    for k in range(TOP_K):
        member = jnp.where(ei == eidx[k:k + 1, :], 1.0, member)
    prefix = _dot(member.astype(BF16), su_ref[...])
    slot = prefix + (pstart_ref[...] + run_ref[:, 0:1])
    rows = [jnp.sum(jnp.where(ei == eidx[k:k + 1, :], slot, 0.0), axis=0, keepdims=True) for k in range(TOP_K)]
    pos_out[...] = jnp.concatenate(rows, axis=0).astype(I32)
    run_ref[...] += jnp.broadcast_to(jnp.sum(member, axis=1, keepdims=True), run_ref.shape)


def _pos_call(eidx, pstart_col, tn):
    K, T = eidx.shape
    r = np.arange(tn)
    su = jnp.asarray(r[:, None] < r[None, :], dtype=BF16)
    return pl.pallas_call(
        _pos_kernel,
        grid=(T // tn,),
        in_specs=[pl.BlockSpec((K, tn), lambda i: (0, i)), pl.BlockSpec((N_EXPERTS, 1), lambda i: (0, 0)),
                  pl.BlockSpec((tn, tn), lambda i: (0, 0))],
        out_specs=pl.BlockSpec((K, tn), lambda i: (0, i)),
        out_shape=jax.ShapeDtypeStruct((K, T), I32),
        scratch_shapes=[pltpu.VMEM((N_EXPERTS, LANES), F32)],
        compiler_params=_cparams(("arbitrary",)),
        name="pos",
    )(eidx, pstart_col, su)


def _dispatch_kernel(td, pos_ref, fin_ref, xs_in, xs_out, sem):
    del xs_in

    def row_copy(t, slot):
        return pltpu.make_async_copy(fin_ref.at[pl.ds(t, 1), :], xs_out.at[pl.ds(slot, 1), :], sem)

    def issue(t, carry):
        for k in range(TOP_K):
            row_copy(t, pos_ref[k, t]).start()
        return carry

    lax.fori_loop(0, td, issue, 0)

    whole = xs_out.at[pl.ds(0, td * TOP_K), :]
    pltpu.make_async_copy(whole, whole, sem).wait()


def _dispatch_call(pos, fin2, xs_zero, td):
    K, T = pos.shape
    D = fin2.shape[1]
    return pl.pallas_call(
        functools.partial(_dispatch_kernel, td),
        grid=(T // td,),
        in_specs=[pl.BlockSpec((K, td), lambda i: (0, i), memory_space=pltpu.SMEM),
                  pl.BlockSpec((td, D), lambda i: (i, 0)),
                  pl.BlockSpec(memory_space=pl.ANY)],
        out_specs=pl.BlockSpec(memory_space=pl.ANY),
        out_shape=jax.ShapeDtypeStruct(xs_zero.shape, xs_zero.dtype),
        scratch_shapes=[pltpu.SemaphoreType.DMA(())],
        input_output_aliases={2: 0},
        compiler_params=_cparams(("arbitrary",)),
        name="dispatch",
    )(pos, fin2, xs_zero)


def _experts_kernel(be_ref, nu_ref, xs_ref, wg_ref, wu_ref, wd_ref, ys_ref, wgu_s, wd_s):
    i = pl.program_id(0)
    prev = be_ref[jnp.maximum(i - 1, 0)]
    changed = jnp.logical_or(i == 0, be_ref[i] != prev)

    @pl.when(jnp.logical_and(changed, i < nu_ref[0]))
    def _():
        wgu_s[:, :EXPERT_FF] = wg_ref[...].astype(BF16)
        wgu_s[:, EXPERT_FF:] = wu_ref[...].astype(BF16)
        wd_s[...] = wd_ref[...].astype(BF16)

    @pl.when(i < nu_ref[0])
    def _():
        xb = xs_ref[...].astype(BF16)
        gu = _dot(xb, wgu_s[...])
        hid = (_silu(gu[:, :EXPERT_FF]) * gu[:, EXPERT_FF:]).astype(BF16)
        ys_ref[...] = _dot(hid, wd_s[...])


def _experts_call(block_e, n_used, xs, wg, wu, wd):
    P, D = xs.shape
    bm = EXPERT_BM
    nb = P // bm

    def xmap(i, be, nu):
        return (jnp.minimum(i, nu[0] - 1), 0)

    def wmap(i, be, nu):
        return (be[i], 0, 0)

    grid_spec = pltpu.PrefetchScalarGridSpec(
        num_scalar_prefetch=2,
        grid=(nb,),
        in_specs=[pl.BlockSpec((bm, D), xmap),
                  pl.BlockSpec((None, D, EXPERT_FF), wmap),
                  pl.BlockSpec((None, D, EXPERT_FF), wmap),
                  pl.BlockSpec((None, EXPERT_FF, D), wmap)],
        out_specs=pl.BlockSpec((bm, D), xmap),
        scratch_shapes=[pltpu.VMEM((D, 2 * EXPERT_FF), BF16), pltpu.VMEM((EXPERT_FF, D), BF16)],
    )
    return pl.pallas_call(
        _experts_kernel,
        grid_spec=grid_spec,
        out_shape=jax.ShapeDtypeStruct((P, D), F32),
        compiler_params=_cparams(("arbitrary",)),
        name="experts",
    )(block_e, n_used, xs, wg, wu, wd)


def _final_kernel(tf, pos_cur, pos_nxt, ys_hbm, w_ref, fin_ref, x1_ref, g2_ref, gpost_ref,
                  wsg_ref, wsu_ref, wsd_ref, o_ref, gbuf, sem):
    i = pl.program_id(0)
    n = pl.num_programs(0)
    slot = i % 2

    def row_copy(pos_ref, t, k, sl):
        return pltpu.make_async_copy(ys_hbm.at[pl.ds(pos_ref[k, t], 1), :],
                                     gbuf.at[sl, k, pl.ds(t, 1), :], sem.at[sl])

    def issue_from(pos_ref, sl):
        def body(t, carry):
            for k in range(TOP_K):
                row_copy(pos_ref, t, k, sl).start()
            return carry
        lax.fori_loop(0, tf, body, 0)

    @pl.when(i == 0)
    def _():
        issue_from(pos_cur, 0)

    @pl.when(i + 1 < n)
    def _():
        issue_from(pos_nxt, 1 - slot)

    pltpu.make_async_copy(gbuf.at[slot], gbuf.at[slot], sem.at[slot]).wait()

    w = w_ref[...]
    eye = lax.broadcasted_iota(I32, (tf, tf), 0) == lax.broadcasted_iota(I32, (tf, tf), 1)
    acc = jnp.zeros((tf, D_MODEL), dtype=F32)
    for k in range(TOP_K):
        w_col = jnp.sum(jnp.where(eye, w[k:k + 1, :], 0.0), axis=1, keepdims=True)
        acc = acc + w_col * gbuf[slot, k]
    fb = fin_ref[...].astype(BF16)
    hid = (_silu(_dot(fb, wsg_ref[...])) * _dot(fb, wsu_ref[...])).astype(BF16)
    fx = acc + _dot(hid, wsd_ref[...])
    o_ref[...] = x1_ref[...] + g2_ref[...] * _rms_rows(fx, gpost_ref[...])


def _final_call(pos, ys, w, fin2, x1f, mod4, gpost, wsg, wsu, wsd, S, tf):
    K, T = pos.shape
    D = D_MODEL
    n = T // tf
    per_b = S // tf

    def full(a):
        nd = a.ndim
        return pl.BlockSpec(a.shape, lambda i: (0,) * nd)

    return pl.pallas_call(
        functools.partial(_final_kernel, tf),
        grid=(n,),
        in_specs=[pl.BlockSpec((K, tf), lambda i: (0, i), memory_space=pltpu.SMEM),
                  pl.BlockSpec((K, tf), lambda i: (0, jnp.minimum(i + 1, n - 1)), memory_space=pltpu.SMEM),
                  pl.BlockSpec(memory_space=pl.ANY),
                  pl.BlockSpec((K, tf), lambda i: (0, i)),
                  pl.BlockSpec((tf, D), lambda i: (i, 0)),
                  pl.BlockSpec((tf, D), lambda i: (i, 0)),
                  pl.BlockSpec((None, None, 1, D), lambda i: (i // per_b, 5, 0, 0)),
                  full(gpost), full(wsg), full(wsu), full(wsd)],
        out_specs=pl.BlockSpec((tf, D), lambda i: (i, 0)),
        out_shape=jax.ShapeDtypeStruct((T, D), F32),
        scratch_shapes=[pltpu.VMEM((2, K, tf, D), F32), pltpu.SemaphoreType.DMA((2,))],
        compiler_params=_cparams(("arbitrary",)),
        name="final",
    )(pos, pos, ys, w, fin2, x1f, mod4, gpost, wsg, wsu, wsd)


def _rope_tables(S):
    nf = ATTN_HEAD_DIM // 4
    t = jnp.arange(S, dtype=jnp.int32)
    pos = jnp.stack([t // GRID_W, t % GRID_W], axis=-1).astype(F32)
    inv_freq = ROPE_THETA ** (-jnp.arange(nf, dtype=F32) / nf)
    ang = pos[:, :, None] * inv_freq
    cos, sin = jnp.cos(ang), jnp.sin(ang)
    cos_h = jnp.concatenate([cos, cos], axis=-1).reshape(S, ATTN_HEAD_DIM)
    sin_h = jnp.concatenate([-sin, sin], axis=-1).reshape(S, ATTN_HEAD_DIM)
    return jnp.tile(cos_h, (1, ATTN_HEADS)), jnp.tile(sin_h, (1, ATTN_HEADS))


def _pick_tile(n, pref):
    t = min(n, pref)
    while n % t:
        t //= 2
    return t


def _layer(x, ctx, cc, l, p):
    B, S, D = x.shape
    CL = ctx.shape[1]
    T = B * S
    NBp = cc.shape[0]

    mod = _ada_call(cc, p["w_ada"][l], p["b_ada"][l][None, :])
    mod4 = mod.reshape(NBp, N_MOD, 1, D)

    w_in = p["w_in"][l]
    o = np.cumsum((0, ATTN_WIDTH, KV_WIDTH, KV_WIDTH, MLSTM_WIDTH, MLSTM_WIDTH, MLSTM_WIDTH, MLSTM_WIDTH,
                   2 * MLSTM_HEADS, 2 * MLSTM_HEADS, D, D))
    half = ATTN_HEADS // 2
    head_order = np.stack([np.arange(half), np.arange(half) + half], axis=1).reshape(-1)
    qperm = (head_order[:, None] * ATTN_HEAD_DIM + np.arange(ATTN_HEAD_DIM)[None, :]).reshape(-1)
    wq = w_in[:, o[0]:o[1]][:, qperm].astype(BF16)
    wkv = w_in[:, o[1]:o[3]].astype(BF16)
    wm = w_in[:, o[3]:o[6]].astype(BF16)
    wo = w_in[:, o[6]:o[7]].astype(BF16)
    w_gates = w_in[:, o[7]:o[9]]
    wif = jnp.pad(w_gates, ((0, 0), (0, LANES - 4 * MLSTM_HEADS))).astype(BF16)
    wift = w_gates.T.astype(BF16)
    wg = w_in[:, o[9]:o[11]].astype(BF16)
    b_gates = jnp.concatenate([p["b_igate"][l], p["b_fgate"][l]]).astype(F32)
    bif = jnp.pad(b_gates, (0, LANES - 4 * MLSTM_HEADS))[None, :]
    bift = b_gates[:, None]
    hidx = np.arange(ATTN_WIDTH) // ATTN_HEAD_DIM
    bd = jnp.asarray(hidx[:, None] == hidx[None, :], dtype=BF16)
    wts = dict(wq=wq, wkv=wkv, wm=wm, wo=wo, wg=wg, wif=wif, wift=wift, bif=bif, bift=bift,
               gq=jnp.tile(p["g_q"][l], ATTN_HEADS)[None, :], gk=jnp.tile(p["g_k"][l], ATTN_KV_HEADS)[None, :],
               bd=bd)
    gpre = p["g_pre_mix"][l][None, :]

    tm = _pick_tile(S, 256)
    px = _inproj_call(False, x, mod4, lambda b: b, gpre, wts, _rope_tables(S), tm)
    pc = _inproj_call(True, ctx, mod4, lambda b: B, gpre, wts, None, _pick_tile(CL, 256))

    a_out = _attn_call(px["q"], pc["k"], px["k"], pc["v"], px["v"], _pick_tile(S, 256))
    hf, hb = _mlstm_call(px, pc, p["conv_w"][l], p["conv_b"][l][None, :])

    mw = dict(gml=p["g_mlstm"][l][None, :], gpost=p["g_post_mix"][l][None, :], gpre2=p["g_pre_ffn"][l][None, :],
              wba=p["w_br_attn"][l][qperm, :].astype(BF16), wbm=p["w_br_mlstm"][l].astype(BF16),
              wout=p["w_out"][l].astype(BF16), wrt=p["w_router"][l].T.astype(BF16))
    x1, fin, st = _merge_call(x, a_out, hf, hb, px["mo"], px["gg"], mod4, mw, tm)

    tn = _pick_tile(T, 512)
    eidx, w, cnt = _route_call(st, p["e_bias"][l].astype(F32)[:, None], tn)
    counts = cnt[:, 0].astype(I32)
    bm = EXPERT_BM
    padded = (counts + bm - 1) // bm * bm
    pend = jnp.cumsum(padded)
    pstart = pend - padded
    n_blocks = (T * TOP_K + N_EXPERTS * (bm - 1) + bm - 1) // bm
    block_start = jnp.arange(n_blocks, dtype=I32) * bm
    block_e = jnp.minimum(jnp.sum((pend[None, :] <= block_start[:, None]).astype(I32), axis=1), N_EXPERTS - 1)
    n_used = (pend[-1:] // bm).astype(I32)
    pos = _pos_call(eidx, pstart.astype(F32)[:, None], tn)

    fin2 = fin.reshape(T, D)
    xs = _dispatch_call(pos, fin2, jnp.zeros((n_blocks * bm, D), F32), _pick_tile(T, 512))
    ys = _experts_call(block_e, n_used, xs, p["w_exp_gate"][l], p["w_exp_up"][l], p["w_exp_down"][l])
    out = _final_call(pos, ys, w, fin2, x1.reshape(T, D), mod4, p["g_post_ffn"][l][None, :],
                      p["w_sh_gate"][l].astype(BF16), p["w_sh_up"][l].astype(BF16),
                      p["w_sh_down"][l].astype(BF16), S, _pick_tile(S, 128))
    return out.reshape(B, S, D)


def kernel(x, c, ctx, c_ctx, w_ada, b_ada, g_pre_mix, g_post_mix, g_pre_ffn, g_post_ffn, w_in, g_q, g_k, conv_w, conv_b, b_igate, b_fgate, g_mlstm, w_br_attn, w_br_mlstm, w_out, w_router, e_bias, w_exp_gate, w_exp_up, w_exp_down, w_sh_gate, w_sh_up, w_sh_down):
    depth = w_ada.shape[0]
    assert depth == 1, "context-token updates (needed only between layers) are not implemented"
    B = x.shape[0]
    nbp = (B + 1 + SUBLANES - 1) // SUBLANES * SUBLANES
    cc = jnp.concatenate([c, c_ctx[None, :], jnp.zeros((nbp - B - 1, c.shape[1]), c.dtype)], axis=0)
    p = dict(w_ada=w_ada, b_ada=b_ada, g_pre_mix=g_pre_mix, g_post_mix=g_post_mix, g_pre_ffn=g_pre_ffn,
             g_post_ffn=g_post_ffn, w_in=w_in, g_q=g_q, g_k=g_k, conv_w=conv_w, conv_b=conv_b, b_igate=b_igate,
             b_fgate=b_fgate, g_mlstm=g_mlstm, w_br_attn=w_br_attn, w_br_mlstm=w_br_mlstm, w_out=w_out,
             w_router=w_router, e_bias=e_bias, w_exp_gate=w_exp_gate, w_exp_up=w_exp_up, w_exp_down=w_exp_down,
             w_sh_gate=w_sh_gate, w_sh_up=w_sh_up, w_sh_down=w_sh_down)
    return _layer(x, ctx, cc, 0, p)
```

```python
import functools

import numpy as np
import jax
import jax.numpy as jnp
from jax import lax
from jax.experimental import pallas as pl
from jax.experimental.pallas import tpu as pltpu

F32 = jnp.float32
BF16 = jnp.bfloat16
I32 = jnp.int32

D_MODEL = 1024
GRID_W = 64
EPS = 1e-6
N_MOD = 6
ATTN_HEADS = 8
ATTN_KV_HEADS = 2
ATTN_HEAD_DIM = 64
ATTN_WIDTH = ATTN_HEADS * ATTN_HEAD_DIM
KV_WIDTH = ATTN_KV_HEADS * ATTN_HEAD_DIM
ROPE_THETA = 10000.0
MLSTM_HEADS = 4
MLSTM_HEAD_DIM = 128
MLSTM_WIDTH = MLSTM_HEADS * MLSTM_HEAD_DIM
CONV_W = 3
N_EXPERTS = 256
TOP_K = 8
N_GROUPS = 8
TOPK_GROUPS = 4
GROUP_SIZE = N_EXPERTS // N_GROUPS
EXPERT_FF = 256
SHARED_FF = 256
ROUTED_SCALE = 2.5

LANES = 128
SUBLANES = 8
VMEM_LIMIT = 56 * 1024 * 1024

MLSTM_L = 128
EXPERT_BM = 256
NEG_INF = float("-inf")


def _cparams(sem):
    return pltpu.CompilerParams(dimension_semantics=sem, vmem_limit_bytes=VMEM_LIMIT)


def _split3(a):
    hi = a.astype(BF16)
    r1 = a - hi.astype(F32)
    mid = r1.astype(BF16)
    lo = (r1 - mid.astype(F32)).astype(BF16)
    return hi, mid, lo


def _dot(a, b):
    return jnp.dot(a, b, preferred_element_type=F32)


def _dot_nt(a, b):
    return lax.dot_general(a, b, (((1,), (1,)), ((), ())), preferred_element_type=F32)


def _dot_tn(a, b):
    return lax.dot_general(a, b, (((0,), (0,)), ((), ())), preferred_element_type=F32)


def _dot3_right(a_f32, t_bf16):
    hi, mid, lo = _split3(a_f32)
    return _dot(hi, t_bf16) + _dot(mid, t_bf16) + _dot(lo, t_bf16)


def _dot3_left(t_bf16, a_f32):
    hi, mid, lo = _split3(a_f32)
    return _dot(t_bf16, hi) + _dot(t_bf16, mid) + _dot(t_bf16, lo)


def _sigmoid(x):
    return 1.0 / (1.0 + jnp.exp(-x))


def _silu(x):
    return x * _sigmoid(x)


def _log_sigmoid(x):
    return jnp.minimum(x, 0.0) - jnp.log(1.0 + jnp.exp(-jnp.abs(x)))


def _rms_rows(x, g):
    return x * lax.rsqrt(jnp.mean(x * x, axis=-1, keepdims=True) + EPS) * g


def _ada_kernel(c_ref, w_ref, b_ref, o_ref):
    a = _silu(c_ref[...]).astype(BF16)
    o_ref[...] = _dot(a, w_ref[...].astype(BF16)) + b_ref[...]


def _ada_call(cc, w_ada, b_ada):
    nb, d = cc.shape
    n = w_ada.shape[1]
    tn = 1536
    return pl.pallas_call(
        _ada_kernel,
        grid=(n // tn,),
        in_specs=[pl.BlockSpec((nb, d), lambda j: (0, 0)),
                  pl.BlockSpec((d, tn), lambda j: (0, j)),
                  pl.BlockSpec((1, tn), lambda j: (0, j))],
        out_specs=pl.BlockSpec((nb, tn), lambda j: (0, j)),
        out_shape=jax.ShapeDtypeStruct((nb, n), F32),
        compiler_params=_cparams(("parallel",)),
        name="ada",
    )(cc, w_ada, b_ada)


def _head_norm(x, g, bd):
    ss = _dot3_right(x * x, bd)
    return x * lax.rsqrt(ss * (1.0 / ATTN_HEAD_DIM) + EPS) * g


def _rope(x, cos, sin_signed):
    w = x.shape[1]
    q = ATTN_HEAD_DIM // 4
    lane = lax.broadcasted_iota(I32, x.shape, 1)
    first = (lane % (2 * q)) < q
    partner = jnp.where(first, pltpu.roll(x, w - q, 1), pltpu.roll(x, q, 1))
    return x * cos + partner * sin_signed


def _inproj_kernel(is_ctx, *refs):
    if is_ctx:
        (x_ref, sh_ref, sc_ref, gpre_ref, wkv_ref, wm_ref, wif_ref, wift_ref, bif_ref, bift_ref,
         gk_ref, bd_ref, k_out, v_out, mqk_out, mv_out, gc_out, gt_out) = refs
    else:
        (x_ref, sh_ref, sc_ref, gpre_ref, wkv_ref, wm_ref, wif_ref, wift_ref, bif_ref, bift_ref,
         gk_ref, bd_ref, wq_ref, wo_ref, wg_ref, gq_ref, cos_ref, sin_ref,
         k_out, v_out, mqk_out, mv_out, gc_out, gt_out, q_out, mo_out, gg_out) = refs
    x = x_ref[...]
    h = _rms_rows(x, gpre_ref[...])
    h = h * (1.0 + sc_ref[...]) + sh_ref[...]
    hb = h.astype(BF16)

    kv = _dot(hb, wkv_ref[...])
    k = _head_norm(kv[:, :KV_WIDTH], gk_ref[...], bd_ref[:KV_WIDTH, :KV_WIDTH])
    if not is_ctx:
        k = _rope(k, cos_ref[:, :KV_WIDTH], sin_ref[:, :KV_WIDTH])
    k_out[...] = k.astype(BF16)
    v_out[...] = kv[:, KV_WIDTH:].astype(BF16)

    m = _dot(hb, wm_ref[...])
    mqk_out[...] = m[:, :2 * MLSTM_WIDTH].astype(BF16)
    mv_out[...] = m[:, 2 * MLSTM_WIDTH:].astype(BF16)

    gc = _dot(hb, wif_ref[...]) + bif_ref[...]
    lane = lax.broadcasted_iota(I32, gc.shape, 1)
    gc_out[...] = jnp.where(lane < 2 * MLSTM_HEADS, gc, _log_sigmoid(gc))
    gt = _dot_nt(wift_ref[...], hb) + bift_ref[...]
    row = lax.broadcasted_iota(I32, gt.shape, 0)
    gt_out[...] = jnp.where(row < 2 * MLSTM_HEADS, gt, _log_sigmoid(gt))

    if not is_ctx:
        q = _dot(hb, wq_ref[...])
        q = _head_norm(q, gq_ref[...], bd_ref[...])
        q = _rope(q, cos_ref[...], sin_ref[...]) * (ATTN_HEAD_DIM ** -0.5)
        q_out[...] = q.astype(BF16)
        mo_out[...] = _dot(hb, wo_ref[...]).astype(BF16)
        gg_out[...] = _dot(hb, wg_ref[...]).astype(BF16)


def _inproj_call(is_ctx, xin, mod4, mod_row_of_batch, gpre, wts, tables, tm):
    B, n, D = xin.shape
    ns = n // tm
    grid = (ns, B)

    def tok(w):
        return pl.BlockSpec((None, tm, w), lambda s, b: (b, s, 0))

    def full(a):
        nd = a.ndim
        return pl.BlockSpec(a.shape, lambda s, b: (0,) * nd)

    def modspec(j):
        return pl.BlockSpec((None, None, 1, D), lambda s, b: (mod_row_of_batch(b), j, 0, 0))

    common = [wts["wkv"], wts["wm"], wts["wif"], wts["wift"], wts["bif"], wts["bift"], wts["gk"], wts["bd"]]
    in_specs = [tok(D), modspec(0), modspec(1), full(gpre)] + [full(a) for a in common]
    args = [xin, mod4, mod4, gpre] + common
    out_shape = [jax.ShapeDtypeStruct((B, n, KV_WIDTH), BF16),
                 jax.ShapeDtypeStruct((B, n, KV_WIDTH), BF16),
                 jax.ShapeDtypeStruct((B, n, 2 * MLSTM_WIDTH), BF16),
                 jax.ShapeDtypeStruct((B, n, MLSTM_WIDTH), BF16),
                 jax.ShapeDtypeStruct((B, n, LANES), F32),
                 jax.ShapeDtypeStruct((B, 4 * MLSTM_HEADS, n), F32)]
    out_specs = [tok(KV_WIDTH), tok(KV_WIDTH), tok(2 * MLSTM_WIDTH), tok(MLSTM_WIDTH), tok(LANES),
                 pl.BlockSpec((None, 4 * MLSTM_HEADS, tm), lambda s, b: (b, 0, s))]
    names = ["k", "v", "mqk", "mv", "gc", "gt"]
    if not is_ctx:
        extra = [wts["wq"], wts["wo"], wts["wg"], wts["gq"]]
        in_specs += [full(a) for a in extra]
        args += extra
        cos, sin = tables
        in_specs += [pl.BlockSpec((tm, ATTN_WIDTH), lambda s, b: (s, 0))] * 2
        args += [cos, sin]
        out_shape += [jax.ShapeDtypeStruct((B, n, ATTN_WIDTH), BF16),
                      jax.ShapeDtypeStruct((B, n, MLSTM_WIDTH), BF16),
                      jax.ShapeDtypeStruct((B, n, 2 * D), BF16)]
        out_specs += [tok(ATTN_WIDTH), tok(MLSTM_WIDTH), tok(2 * D)]
        names += ["q", "mo", "gg"]
    outs = pl.pallas_call(
        functools.partial(_inproj_kernel, is_ctx),
        grid=grid, in_specs=in_specs, out_specs=out_specs, out_shape=out_shape,
        compiler_params=_cparams(("parallel", "parallel")),
        name="inproj_ctx" if is_ctx else "inproj",
    )(*args)
    return dict(zip(names, outs))


def _attn_kernel(q_ref, kc_ref, kx_ref, vc_ref, vx_ref, o_ref):
    kc, kx, vc, vx = kc_ref[...], kx_ref[...], vc_ref[...], vx_ref[...]
    tq = q_ref.shape[0]
    lane = lax.broadcasted_iota(I32, (tq, LANES), 1)
    low = lane < ATTN_HEAD_DIM
    n_slab = ATTN_WIDTH // LANES
    for j in range(n_slab):
        slab = q_ref[:, j * LANES:(j + 1) * LANES]
        zs = []
        for part in range(2):
            qh = jnp.where(low if part == 0 else jnp.logical_not(low), slab, jnp.zeros_like(slab))
            sc = _dot_nt(qh, kc)
            sx = _dot_nt(qh, kx)
            m = jnp.maximum(jnp.max(sc, axis=1, keepdims=True), jnp.max(sx, axis=1, keepdims=True))
            pc = jnp.exp(sc - m)
            px = jnp.exp(sx - m)
            l = jnp.sum(pc, axis=1, keepdims=True) + jnp.sum(px, axis=1, keepdims=True)
            z = _dot(pc.astype(BF16), vc) + _dot(px.astype(BF16), vx)
            zs.append(z / l)
        o_ref[:, j * LANES:(j + 1) * LANES] = jnp.where(low, zs[0], zs[1]).astype(BF16)


def _attn_call(q, kc, kx, vc, vx, tq):
    B, S, _ = q.shape
    CL = kc.shape[1]

    def seq(n):
        return pl.BlockSpec((None, n, KV_WIDTH), lambda b, s: (b, 0, 0))

    return pl.pallas_call(
        _attn_kernel,
        grid=(B, S // tq),
        in_specs=[pl.BlockSpec((None, tq, ATTN_WIDTH), lambda b, s: (b, s, 0)),
                  seq(CL), seq(S), seq(CL), seq(S)],
        out_specs=pl.BlockSpec((None, tq, ATTN_WIDTH), lambda b, s: (b, s, 0)),
        out_shape=jax.ShapeDtypeStruct((B, S, ATTN_WIDTH), BF16),
        compiler_params=_cparams(("parallel", "parallel")),
        name="attn",
    )(q, kc, kx, vc, vx)


def _mlstm_kernel(n_chunks_x, n_chunks_c,
                  mqk_x, mv_x, gc_x, gt_x, mqk_c, mv_c, gc_c, gt_c, convw, convb, tl_ref, tu_ref,
                  hf_out, hb_out, qs_x, ks_x, qs_c, ks_c, ct_ref, m_ref):
    L = MLSTM_L
    H = MLSTM_HEADS
    W = MLSTM_WIDTH
    hd = MLSTM_HEAD_DIM
    tl = tl_ref[...]
    tu = tu_ref[...]
    w0, w1, w2, cb = convw[0:1, :], convw[1:2, :], convw[2:3, :], convb[...]
    rows = lax.broadcasted_iota(I32, (L, 2 * W), 0)
    lane2 = lax.broadcasted_iota(I32, (L, 2 * W), 1)
    edge = 2 * SUBLANES

    def conv_chunk(src, nch, qdst, kdst, i):
        start = pl.multiple_of(i * L, L)
        a = src[pl.ds(start, L), :].astype(F32)
        pstart = pl.multiple_of(jnp.maximum(start - edge, 0), edge)
        prev_row = src[pl.ds(pstart, edge), :][edge - 1:edge, :].astype(F32)
        prev_row = jnp.where(i > 0, prev_row, jnp.zeros_like(prev_row))
        nstart = pl.multiple_of(jnp.minimum(start + L, (nch - 1) * L), edge)
        next_row = src[pl.ds(nstart, edge), :][0:1, :].astype(F32)
        next_row = jnp.where(i < nch - 1, next_row, jnp.zeros_like(next_row))
        a_prev = jnp.where(rows == 0, prev_row, pltpu.roll(a, 1, 0))
        a_next = jnp.where(rows == L - 1, next_row, pltpu.roll(a, L - 1, 0))
        y = _silu(w0 * a_prev + w1 * a + w2 * a_next + cb)
        y = jnp.where(lane2 < W, y * (hd ** -0.5), y)
        qdst[pl.ds(start, L), :] = y[:, :W].astype(BF16)
        kdst[pl.ds(start, L), :] = y[:, W:].astype(BF16)

    def conv_c(i, carry):
        conv_chunk(mqk_c, n_chunks_c, qs_c, ks_c, i)
        return carry

    def conv_x(i, carry):
        conv_chunk(mqk_x, n_chunks_x, qs_x, ks_x, i)
        return carry

    lax.fori_loop(0, n_chunks_c, conv_c, 0)
    lax.fori_loop(0, n_chunks_x, conv_x, 0)

    ct_ref[...] = jnp.zeros_like(ct_ref)
    m_ref[...] = jnp.zeros_like(m_ref)

    ri = lax.broadcasted_iota(I32, (L, L), 0)
    ci = lax.broadcasted_iota(I32, (L, L), 1)
    vlane = lax.broadcasted_iota(I32, (L, hd), 1)
    ones_col = jnp.where(vlane == 0, 1.0, 0.0).astype(BF16)

    def chunk_step(qs, ks, mv, gc_ref, gt_ref, start, direction, h_out):
        gc = gc_ref[pl.ds(start, L), :]
        gt = gt_ref[:, pl.ds(start, L)]
        if direction == 0:
            bcol = _dot3_left(tl, gc)
            brow = _dot3_right(gt, tu)
            mask = ci <= ri
        else:
            bcol = _dot3_left(tu, gc)
            brow = _dot3_right(gt, tl)
            mask = ci >= ri
        for hh in range(H):
            c = direction * H + hh
            q_c = qs[pl.ds(start, L), hh * hd:(hh + 1) * hd]
            k_c = ks[pl.ds(start, L), hh * hd:(hh + 1) * hd]
            v_c = mv[pl.ds(start, L), hh * hd:(hh + 1) * hd]
            vaug = jnp.concatenate([v_c, ones_col], axis=1)
            li_col = gc[:, c:c + 1]
            li_row = gt[c:c + 1, :]
            b_col = bcol[:, 2 * H + c:2 * H + c + 1]
            b_row = brow[2 * H + c:2 * H + c + 1, :]
            if direction == 0:
                b_tot = b_row[:, L - 1:L]
            else:
                b_tot = b_row[:, 0:1]
            m_old = m_ref[c:c + 1, 0:1]
            ct = ct_ref[c]
            if h_out is not None:
                dm = jnp.where(mask, b_col - b_row + li_row, NEG_INF)
                inter = b_col + m_old
                m_q = jnp.maximum(inter, jnp.max(dm, axis=1, keepdims=True))
                wmat = jnp.exp(dm - m_q)
                w_inter = jnp.exp(inter - m_q)
                p = (_dot_nt(q_c, k_c) * wmat).astype(BF16)
                r = w_inter * _dot(q_c, ct.astype(BF16)) + _dot(p, vaug)
                den = jnp.maximum(jnp.abs(r[:, hd:hd + 1]), jnp.exp(-m_q))
                h_out[pl.ds(start, L), hh * hd:(hh + 1) * hd] = (r[:, :hd] / den).astype(BF16)
            dec = b_tot - b_col + li_col
            m_new = jnp.maximum(b_tot + m_old, jnp.max(dec, axis=0, keepdims=True))
            w_s = jnp.exp(dec - m_new)
            w_old = jnp.exp(b_tot + m_old - m_new)
            kw = (k_c.astype(F32) * w_s).astype(BF16)
            ct_ref[c] = w_old * ct + _dot_tn(kw, vaug)
            m_ref[c:c + 1, :] = jnp.broadcast_to(m_new, (1, LANES))

    def ctx_body(i, carry):
        chunk_step(qs_c, ks_c, mv_c, gc_c, gt_c, pl.multiple_of(i * L, L), 0, None)
        chunk_step(qs_c, ks_c, mv_c, gc_c, gt_c, pl.multiple_of((n_chunks_c - 1 - i) * L, L), 1, None)
        return carry

    lax.fori_loop(0, n_chunks_c, ctx_body, 0)

    def x_body(i, carry):
        chunk_step(qs_x, ks_x, mv_x, gc_x, gt_x, pl.multiple_of(i * L, L), 0, hf_out)
        chunk_step(qs_x, ks_x, mv_x, gc_x, gt_x, pl.multiple_of((n_chunks_x - 1 - i) * L, L), 1, hb_out)
        return carry

    lax.fori_loop(0, n_chunks_x, x_body, 0)


def _mlstm_call(px, pc, convw, convb):
    B, S, _ = px["mqk"].shape
    CL = pc["mqk"].shape[1]
    L = MLSTM_L
    W = MLSTM_WIDTH
    r = np.arange(L)
    tl = jnp.asarray(r[None, :] <= r[:, None], dtype=BF16)
    tu = jnp.asarray(r[None, :] >= r[:, None], dtype=BF16)

    def seq(n, w):
        return pl.BlockSpec((None, n, w), lambda b: (b, 0, 0))

    def gts(n):
        return pl.BlockSpec((None, 4 * MLSTM_HEADS, n), lambda b: (b, 0, 0))

    def full(a):
        nd = a.ndim
        return pl.BlockSpec(a.shape, lambda b: (0,) * nd)

    return pl.pallas_call(
        functools.partial(_mlstm_kernel, S // L, CL // L),
        grid=(B,),
        in_specs=[seq(S, 2 * W), seq(S, W), seq(S, LANES), gts(S),
                  seq(CL, 2 * W), seq(CL, W), seq(CL, LANES), gts(CL),
                  full(convw), full(convb), full(tl), full(tu)],
        out_specs=[seq(S, W), seq(S, W)],
        out_shape=[jax.ShapeDtypeStruct((B, S, W), BF16)] * 2,
        scratch_shapes=[pltpu.VMEM((S, W), BF16), pltpu.VMEM((S, W), BF16),
                        pltpu.VMEM((CL, W), BF16), pltpu.VMEM((CL, W), BF16),
                        pltpu.VMEM((2 * MLSTM_HEADS, MLSTM_HEAD_DIM, 2 * MLSTM_HEAD_DIM), F32),
                        pltpu.VMEM((2 * MLSTM_HEADS, LANES), F32)],
        compiler_params=_cparams(("parallel",)),
        name="mlstm",
    )(px["mqk"], px["mv"], px["gc"], px["gt"], pc["mqk"], pc["mv"], pc["gc"], pc["gt"], convw, convb, tl, tu)


def _merge_kernel(x_ref, a_ref, hf_ref, hb_ref, mo_ref, gg_ref, g1_ref, sh2_ref, sc2_ref,
                  gml_ref, gpost_ref, gpre_ref, wba_ref, wbm_ref, wout_ref, wrt_ref,
                  x1_out, fin_out, st_out):
    D = D_MODEL
    hs = hf_ref[...].astype(F32) + hb_ref[...].astype(F32)
    gml = gml_ref[...]
    parts = []
    for hh in range(MLSTM_HEADS):
        sl = slice(hh * MLSTM_HEAD_DIM, (hh + 1) * MLSTM_HEAD_DIM)
        parts.append(_rms_rows(hs[:, sl], gml[:, sl]))
    hn = jnp.concatenate(parts, axis=1)
    m_out = (hn * _sigmoid(mo_ref[...].astype(F32))).astype(BF16)
    gg = gg_ref[...].astype(F32)
    y = _sigmoid(gg[:, :D]) * _dot(a_ref[...], wba_ref[...]) + _sigmoid(gg[:, D:]) * _dot(m_out, wbm_ref[...])
    yx = _dot(y.astype(BF16), wout_ref[...])
    x1 = x_ref[...] + g1_ref[...] * _rms_rows(yx, gpost_ref[...])
    x1_out[...] = x1
    fin = _rms_rows(x1, gpre_ref[...]) * (1.0 + sc2_ref[...]) + sh2_ref[...]
    fin_out[...] = fin
    st_out[...] = _sigmoid(_dot_nt(wrt_ref[...], fin.astype(BF16)))


def _merge_call(x, a_out, hf, hb, mo, gg, mod4, wts, tm):
    B, S, D = x.shape
    ns = S // tm

    def tok(w):
        return pl.BlockSpec((None, tm, w), lambda s, b: (b, s, 0))

    def full(a):
        nd = a.ndim
        return pl.BlockSpec(a.shape, lambda s, b: (0,) * nd)

    def modspec(j):
        return pl.BlockSpec((None, None, 1, D), lambda s, b: (b, j, 0, 0))

    consts = [wts["gml"], wts["gpost"], wts["gpre2"], wts["wba"], wts["wbm"], wts["wout"], wts["wrt"]]
    return pl.pallas_call(
        _merge_kernel,
        grid=(ns, B),
        in_specs=[tok(D), tok(ATTN_WIDTH), tok(MLSTM_WIDTH), tok(MLSTM_WIDTH), tok(MLSTM_WIDTH), tok(2 * D),
                  modspec(2), modspec(3), modspec(4)] + [full(a) for a in consts],
        out_specs=[tok(D), tok(D), pl.BlockSpec((N_EXPERTS, tm), lambda s, b: (0, b * ns + s))],
        out_shape=[jax.ShapeDtypeStruct((B, S, D), F32), jax.ShapeDtypeStruct((B, S, D), F32),
                   jax.ShapeDtypeStruct((N_EXPERTS, B * S), F32)],
        compiler_params=_cparams(("parallel", "parallel")),
        name="merge",
    )(x, a_out, hf, hb, mo, gg, mod4, mod4, mod4, *consts)


def _first_argmax(v, idx, big):
    m = jnp.max(v, axis=0, keepdims=True)
    first = jnp.min(jnp.where(v == m, idx, big), axis=0, keepdims=True)
    return m, first


def _route_kernel(st_ref, bias_ref, eidx_out, w_out, cnt_out):
    s = st_ref[...]
    E, tn = s.shape
    sel = s + bias_ref[...]
    gi = lax.broadcasted_iota(I32, (GROUP_SIZE, tn), 0)
    gscores = []
    for g in range(N_GROUPS):
        sg = sel[g * GROUP_SIZE:(g + 1) * GROUP_SIZE, :]
        m1, i1 = _first_argmax(sg, gi, GROUP_SIZE)
        m2 = jnp.max(jnp.where(gi == i1, NEG_INF, sg), axis=0, keepdims=True)
        gscores.append(m1 + m2)
    gs = jnp.concatenate(gscores, axis=0)
    gidx = lax.broadcasted_iota(I32, (N_GROUPS, tn), 0)
    chosen = jnp.zeros((N_GROUPS, tn), dtype=F32)
    for _ in range(TOPK_GROUPS):
        _, ig = _first_argmax(gs, gidx, N_GROUPS)
        hit = gidx == ig
        chosen = jnp.where(hit, 1.0, chosen)
        gs = jnp.where(hit, NEG_INF, gs)
    masked = jnp.concatenate(
        [jnp.where(chosen[g:g + 1, :] > 0.0, sel[g * GROUP_SIZE:(g + 1) * GROUP_SIZE, :], NEG_INF)
         for g in range(N_GROUPS)], axis=0)
    ei = lax.broadcasted_iota(I32, (E, tn), 0)
    idxs, ws = [], []
    member = jnp.zeros((E, tn), dtype=F32)
    for _ in range(TOP_K):
        _, ie = _first_argmax(masked, ei, E)
        hit = ei == ie
        ws.append(jnp.sum(jnp.where(hit, s, 0.0), axis=0, keepdims=True))
        idxs.append(ie)
        member = jnp.where(hit, 1.0, member)
        masked = jnp.where(hit, NEG_INF, masked)
    w = jnp.concatenate(ws, axis=0)
    w = w / jnp.sum(w, axis=0, keepdims=True) * ROUTED_SCALE
    eidx_out[...] = jnp.concatenate(idxs, axis=0)
    w_out[...] = w

    @pl.when(pl.program_id(0) == 0)
    def _():
        cnt_out[...] = jnp.zeros_like(cnt_out)

    cnt_out[...] += jnp.broadcast_to(jnp.sum(member, axis=1, keepdims=True), cnt_out.shape)


def _route_call(st, e_bias_col, tn):
    E, T = st.shape
    return pl.pallas_call(
        _route_kernel,
        grid=(T // tn,),
        in_specs=[pl.BlockSpec((E, tn), lambda i: (0, i)), pl.BlockSpec((E, 1), lambda i: (0, 0))],
        out_specs=[pl.BlockSpec((TOP_K, tn), lambda i: (0, i)), pl.BlockSpec((TOP_K, tn), lambda i: (0, i)),
                   pl.BlockSpec((E, LANES), lambda i: (0, 0))],
        out_shape=[jax.ShapeDtypeStruct((TOP_K, T), I32), jax.ShapeDtypeStruct((TOP_K, T), F32),
                   jax.ShapeDtypeStruct((E, LANES), F32)],
        compiler_params=_cparams(("arbitrary",)),
        name="route",
    )(st, e_bias_col)


def _pos_kernel(eidx_ref, pstart_ref, su_ref, pos_out, run_ref):
    @pl.when(pl.program_id(0) == 0)
    def _():
        run_ref[...] = jnp.zeros_like(run_ref)

    eidx = eidx_ref[...]
    tn = eidx.shape[1]
    E = N_EXPERTS
    ei = lax.broadcasted_iota(I32, (E, tn), 0)
    member = jnp.zeros((E, tn), dtype=F32)
    for k in range(TOP_K):
        member = jnp.where(ei == eidx[k:k + 1, :], 1.0, member)
    prefix = _dot(member.astype(BF16), su_ref[...])
    slot = prefix + (pstart_ref[...] + run_ref[:, 0:1])
    rows = [jnp.sum(jnp.where(ei == eidx[k:k + 1, :], slot, 0.0), axis=0, keepdims=True) for k in range(TOP_K)]
    pos_out[...] = jnp.concatenate(rows, axis=0).astype(I32)
    run_ref[...] += jnp.broadcast_to(jnp.sum(member, axis=1, keepdims=True), run_ref.shape)


def _pos_call(eidx, pstart_col, tn):
    K, T = eidx.shape
    r = np.arange(tn)
    su = jnp.asarray(r[:, None] < r[None, :], dtype=BF16)
    return pl.pallas_call(
        _pos_kernel,
        grid=(T // tn,),
        in_specs=[pl.BlockSpec((K, tn), lambda i: (0, i)), pl.BlockSpec((N_EXPERTS, 1), lambda i: (0, 0)),
                  pl.BlockSpec((tn, tn), lambda i: (0, 0))],
        out_specs=pl.BlockSpec((K, tn), lambda i: (0, i)),
        out_shape=jax.ShapeDtypeStruct((K, T), I32),
        scratch_shapes=[pltpu.VMEM((N_EXPERTS, LANES), F32)],
        compiler_params=_cparams(("arbitrary",)),
        name="pos",
    )(eidx, pstart_col, su)


def _dispatch_kernel(td, pos_ref, fin_ref, xs_in, xs_out, sem):
    del xs_in

    def row_copy(t, slot):
        return pltpu.make_async_copy(fin_ref.at[pl.ds(t, 1), :], xs_out.at[pl.ds(slot, 1), :], sem)

    def issue(t, carry):
        for k in range(TOP_K):
            row_copy(t, pos_ref[k, t]).start()
        return carry

    lax.fori_loop(0, td, issue, 0)

    whole = xs_out.at[pl.ds(0, td * TOP_K), :]
    pltpu.make_async_copy(whole, whole, sem).wait()


def _dispatch_call(pos, fin2, xs_zero, td):
    K, T = pos.shape
    D = fin2.shape[1]
    return pl.pallas_call(
        functools.partial(_dispatch_kernel, td),
        grid=(T // td,),
        in_specs=[pl.BlockSpec((K, td), lambda i: (0, i), memory_space=pltpu.SMEM),
                  pl.BlockSpec((td, D), lambda i: (i, 0)),
                  pl.BlockSpec(memory_space=pl.ANY)],
        out_specs=pl.BlockSpec(memory_space=pl.ANY),
        out_shape=jax.ShapeDtypeStruct(xs_zero.shape, xs_zero.dtype),
        scratch_shapes=[pltpu.SemaphoreType.DMA(())],
        input_output_aliases={2: 0},
        compiler_params=_cparams(("arbitrary",)),
        name="dispatch",
    )(pos, fin2, xs_zero)


def _experts_kernel(be_ref, nu_ref, xs_ref, wg_ref, wu_ref, wd_ref, ys_ref, wgu_s, wd_s):
    i = pl.program_id(0)
    prev = be_ref[jnp.maximum(i - 1, 0)]
    changed = jnp.logical_or(i == 0, be_ref[i] != prev)

    @pl.when(jnp.logical_and(changed, i < nu_ref[0]))
    def _():
        wgu_s[:, :EXPERT_FF] = wg_ref[...].astype(BF16)
        wgu_s[:, EXPERT_FF:] = wu_ref[...].astype(BF16)
        wd_s[...] = wd_ref[...].astype(BF16)

    @pl.when(i < nu_ref[0])
    def _():
        xb = xs_ref[...].astype(BF16)
        gu = _dot(xb, wgu_s[...])
        hid = (_silu(gu[:, :EXPERT_FF]) * gu[:, EXPERT_FF:]).astype(BF16)
        ys_ref[...] = _dot(hid, wd_s[...])


def _experts_call(block_e, n_used, xs, wg, wu, wd):
    P, D = xs.shape
    bm = EXPERT_BM
    nb = P // bm

    def xmap(i, be, nu):
        return (jnp.minimum(i, nu[0] - 1), 0)

    def wmap(i, be, nu):
        return (be[i], 0, 0)

    grid_spec = pltpu.PrefetchScalarGridSpec(
        num_scalar_prefetch=2,
        grid=(nb,),
        in_specs=[pl.BlockSpec((bm, D), xmap),
                  pl.BlockSpec((None, D, EXPERT_FF), wmap),
                  pl.BlockSpec((None, D, EXPERT_FF), wmap),
                  pl.BlockSpec((None, EXPERT_FF, D), wmap)],
        out_specs=pl.BlockSpec((bm, D), xmap),
        scratch_shapes=[pltpu.VMEM((D, 2 * EXPERT_FF), BF16), pltpu.VMEM((EXPERT_FF, D), BF16)],
    )
    return pl.pallas_call(
        _experts_kernel,
        grid_spec=grid_spec,
        out_shape=jax.ShapeDtypeStruct((P, D), F32),
        compiler_params=_cparams(("arbitrary",)),
        name="experts",
    )(block_e, n_used, xs, wg, wu, wd)


def _final_kernel(tf, pos_cur, pos_nxt, ys_hbm, w_ref, fin_ref, x1_ref, g2_ref, gpost_ref,
                  wsg_ref, wsu_ref, wsd_ref, o_ref, gbuf, sem):
    i = pl.program_id(0)
    n = pl.num_programs(0)
    slot = i % 2

    def row_copy(pos_ref, t, k, sl):
        return pltpu.make_async_copy(ys_hbm.at[pl.ds(pos_ref[k, t], 1), :],
                                     gbuf.at[sl, k, pl.ds(t, 1), :], sem.at[sl])

    def issue_from(pos_ref, sl):
        def body(t, carry):
            for k in range(TOP_K):
                row_copy(pos_ref, t, k, sl).start()
            return carry
        lax.fori_loop(0, tf, body, 0)

    @pl.when(i == 0)
    def _():
        issue_from(pos_cur, 0)

    def step(cur):
        @pl.when(i + 1 < n)
        def _():
            issue_from(pos_nxt, 1 - cur)

        pltpu.make_async_copy(gbuf.at[cur], gbuf.at[cur], sem.at[cur]).wait()

        w = w_ref[...]
        eye = lax.broadcasted_iota(I32, (tf, tf), 0) == lax.broadcasted_iota(I32, (tf, tf), 1)
        acc = jnp.zeros((tf, D_MODEL), dtype=F32)
        for k in range(TOP_K):
            w_col = jnp.sum(jnp.where(eye, w[k:k + 1, :], 0.0), axis=1, keepdims=True)
            acc = acc + w_col * gbuf[cur, k]
        fb = fin_ref[...].astype(BF16)
        hid = (_silu(_dot(fb, wsg_ref[...])) * _dot(fb, wsu_ref[...])).astype(BF16)
        fx = acc + _dot(hid, wsd_ref[...])
        o_ref[...] = x1_ref[...] + g2_ref[...] * _rms_rows(fx, gpost_ref[...])

    @pl.when(slot == 0)
    def _():
        step(0)

    @pl.when(slot == 1)
    def _():
        step(1)


def _final_call(pos, ys, w, fin2, x1f, mod4, gpost, wsg, wsu, wsd, S, tf):
    K, T = pos.shape
    D = D_MODEL
    n = T // tf
    per_b = S // tf

    def full(a):
        nd = a.ndim
        return pl.BlockSpec(a.shape, lambda i: (0,) * nd)

    return pl.pallas_call(
        functools.partial(_final_kernel, tf),
        grid=(n,),
        in_specs=[pl.BlockSpec((K, tf), lambda i: (0, i), memory_space=pltpu.SMEM),
                  pl.BlockSpec((K, tf), lambda i: (0, jnp.minimum(i + 1, n - 1)), memory_space=pltpu.SMEM),
                  pl.BlockSpec(memory_space=pl.ANY),
                  pl.BlockSpec((K, tf), lambda i: (0, i)),
                  pl.BlockSpec((tf, D), lambda i: (i, 0)),
                  pl.BlockSpec((tf, D), lambda i: (i, 0)),
                  pl.BlockSpec((None, None, 1, D), lambda i: (i // per_b, 5, 0, 0)),
                  full(gpost), full(wsg), full(wsu), full(wsd)],
        out_specs=pl.BlockSpec((tf, D), lambda i: (i, 0)),
        out_shape=jax.ShapeDtypeStruct((T, D), F32),
        scratch_shapes=[pltpu.VMEM((2, K, tf, D), F32), pltpu.SemaphoreType.DMA((2,))],
        compiler_params=_cparams(("arbitrary",)),
        name="final",
    )(pos, pos, ys, w, fin2, x1f, mod4, gpost, wsg, wsu, wsd)


def _rope_tables(S):
    nf = ATTN_HEAD_DIM // 4
    t = jnp.arange(S, dtype=jnp.int32)
    pos = jnp.stack([t // GRID_W, t % GRID_W], axis=-1).astype(F32)
    inv_freq = ROPE_THETA ** (-jnp.arange(nf, dtype=F32) / nf)
    ang = pos[:, :, None] * inv_freq
    cos, sin = jnp.cos(ang), jnp.sin(ang)
    cos_h = jnp.concatenate([cos, cos], axis=-1).reshape(S, ATTN_HEAD_DIM)
    sin_h = jnp.concatenate([-sin, sin], axis=-1).reshape(S, ATTN_HEAD_DIM)
    return jnp.tile(cos_h, (1, ATTN_HEADS)), jnp.tile(sin_h, (1, ATTN_HEADS))


def _pick_tile(n, pref):
    t = min(n, pref)
    while n % t:
        t //= 2
    return t


def _layer(x, ctx, cc, l, p):
    B, S, D = x.shape
    CL = ctx.shape[1]
    T = B * S
    NBp = cc.shape[0]

    mod = _ada_call(cc, p["w_ada"][l], p["b_ada"][l][None, :])
    mod4 = mod.reshape(NBp, N_MOD, 1, D)

    w_in = p["w_in"][l]
    o = np.cumsum((0, ATTN_WIDTH, KV_WIDTH, KV_WIDTH, MLSTM_WIDTH, MLSTM_WIDTH, MLSTM_WIDTH, MLSTM_WIDTH,
                   2 * MLSTM_HEADS, 2 * MLSTM_HEADS, D, D))
    half = ATTN_HEADS // 2
    head_order = np.stack([np.arange(half), np.arange(half) + half], axis=1).reshape(-1)
    qperm = (head_order[:, None] * ATTN_HEAD_DIM + np.arange(ATTN_HEAD_DIM)[None, :]).reshape(-1)
    wq = w_in[:, o[0]:o[1]][:, qperm].astype(BF16)
    wkv = w_in[:, o[1]:o[3]].astype(BF16)
    wm = w_in[:, o[3]:o[6]].astype(BF16)
    wo = w_in[:, o[6]:o[7]].astype(BF16)
    w_gates = w_in[:, o[7]:o[9]]
    wif = jnp.pad(w_gates, ((0, 0), (0, LANES - 4 * MLSTM_HEADS))).astype(BF16)
    wift = w_gates.T.astype(BF16)
    wg = w_in[:, o[9]:o[11]].astype(BF16)
    b_gates = jnp.concatenate([p["b_igate"][l], p["b_fgate"][l]]).astype(F32)
    bif = jnp.pad(b_gates, (0, LANES - 4 * MLSTM_HEADS))[None, :]
    bift = b_gates[:, None]
    hidx = np.arange(ATTN_WIDTH) // ATTN_HEAD_DIM
    bd = jnp.asarray(hidx[:, None] == hidx[None, :], dtype=BF16)
    wts = dict(wq=wq, wkv=wkv, wm=wm, wo=wo, wg=wg, wif=wif, wift=wift, bif=bif, bift=bift,
               gq=jnp.tile(p["g_q"][l], ATTN_HEADS)[None, :], gk=jnp.tile(p["g_k"][l], ATTN_KV_HEADS)[None, :],
               bd=bd)
    gpre = p["g_pre_mix"][l][None, :]

    tm = _pick_tile(S, 256)
    px = _inproj_call(False, x, mod4, lambda b: b, gpre, wts, _rope_tables(S), tm)
    pc = _inproj_call(True, ctx, mod4, lambda b: B, gpre, wts, None, _pick_tile(CL, 256))

    a_out = _attn_call(px["q"], pc["k"], px["k"], pc["v"], px["v"], _pick_tile(S, 256))
    hf, hb = _mlstm_call(px, pc, p["conv_w"][l], p["conv_b"][l][None, :])

    mw = dict(gml=p["g_mlstm"][l][None, :], gpost=p["g_post_mix"][l][None, :], gpre2=p["g_pre_ffn"][l][None, :],
              wba=p["w_br_attn"][l][qperm, :].astype(BF16), wbm=p["w_br_mlstm"][l].astype(BF16),
              wout=p["w_out"][l].astype(BF16), wrt=p["w_router"][l].T.astype(BF16))
    x1, fin, st = _merge_call(x, a_out, hf, hb, px["mo"], px["gg"], mod4, mw, tm)

    tn = _pick_tile(T, 512)
    eidx, w, cnt = _route_call(st, p["e_bias"][l].astype(F32)[:, None], tn)
    counts = cnt[:, 0].astype(I32)
    bm = EXPERT_BM
    padded = (counts + bm - 1) // bm * bm
    pend = jnp.cumsum(padded)
    pstart = pend - padded
    n_blocks = (T * TOP_K + N_EXPERTS * (bm - 1) + bm - 1) // bm
    block_start = jnp.arange(n_blocks, dtype=I32) * bm
    block_e = jnp.minimum(jnp.sum((pend[None, :] <= block_start[:, None]).astype(I32), axis=1), N_EXPERTS - 1)
    n_used = (pend[-1:] // bm).astype(I32)
    pos = _pos_call(eidx, pstart.astype(F32)[:, None], tn)

    fin2 = fin.reshape(T, D)
    xs = _dispatch_call(pos, fin2, jnp.zeros((n_blocks * bm, D), F32), _pick_tile(T, 512))
    ys = _experts_call(block_e, n_used, xs, p["w_exp_gate"][l], p["w_exp_up"][l], p["w_exp_down"][l])
    out = _final_call(pos, ys, w, fin2, x1.reshape(T, D), mod4, p["g_post_ffn"][l][None, :],
                      p["w_sh_gate"][l].astype(BF16), p["w_sh_up"][l].astype(BF16),
                      p["w_sh_down"][l].astype(BF16), S, _pick_tile(S, 128))
    return out.reshape(B, S, D)


def kernel(x, c, ctx, c_ctx, w_ada, b_ada, g_pre_mix, g_post_mix, g_pre_ffn, g_post_ffn, w_in, g_q, g_k, conv_w, conv_b, b_igate, b_fgate, g_mlstm, w_br_attn, w_br_mlstm, w_out, w_router, e_bias, w_exp_gate, w_exp_up, w_exp_down, w_sh_gate, w_sh_up, w_sh_down):
    depth = w_ada.shape[0]
    assert depth == 1, "context-token updates (needed only between layers) are not implemented"
    B = x.shape[0]
    nbp = (B + 1 + SUBLANES - 1) // SUBLANES * SUBLANES
    cc = jnp.concatenate([c, c_ctx[None, :], jnp.zeros((nbp - B - 1, c.shape[1]), c.dtype)], axis=0)
    p = dict(w_ada=w_ada, b_ada=b_ada, g_pre_mix=g_pre_mix, g_post_mix=g_post_mix, g_pre_ffn=g_pre_ffn,
             g_post_ffn=g_post_ffn, w_in=w_in, g_q=g_q, g_k=g_k, conv_w=conv_w, conv_b=conv_b, b_igate=b_igate,
             b_fgate=b_fgate, g_mlstm=g_mlstm, w_br_attn=w_br_attn, w_br_mlstm=w_br_mlstm, w_out=w_out,
             w_router=w_router, e_bias=e_bias, w_exp_gate=w_exp_gate, w_exp_up=w_exp_up, w_exp_down=w_exp_down,
             w_sh_gate=w_sh_gate, w_sh_up=w_sh_up, w_sh_down=w_sh_down)
    return _layer(x, ctx, cc, 0, p)
```

```python
import functools

import numpy as np
import jax
import jax.numpy as jnp
from jax import lax
from jax.experimental import pallas as pl
from jax.experimental.pallas import tpu as pltpu

F32 = jnp.float32
BF16 = jnp.bfloat16
I32 = jnp.int32

D_MODEL = 1024
GRID_W = 64
EPS = 1e-6
N_MOD = 6
ATTN_HEADS = 8
ATTN_KV_HEADS = 2
ATTN_HEAD_DIM = 64
ATTN_WIDTH = ATTN_HEADS * ATTN_HEAD_DIM
KV_WIDTH = ATTN_KV_HEADS * ATTN_HEAD_DIM
ROPE_THETA = 10000.0
MLSTM_HEADS = 4
MLSTM_HEAD_DIM = 128
MLSTM_WIDTH = MLSTM_HEADS * MLSTM_HEAD_DIM
CONV_W = 3
N_EXPERTS = 256
TOP_K = 8
N_GROUPS = 8
TOPK_GROUPS = 4
GROUP_SIZE = N_EXPERTS // N_GROUPS
EXPERT_FF = 256
SHARED_FF = 256
ROUTED_SCALE = 2.5

LANES = 128
SUBLANES = 8
VMEM_LIMIT = 56 * 1024 * 1024

MLSTM_L = 128
EXPERT_BM = 256
NEG_INF = float("-inf")


def _cparams(sem):
    return pltpu.CompilerParams(dimension_semantics=sem, vmem_limit_bytes=VMEM_LIMIT)


def _split3(a):
    hi = a.astype(BF16)
    r1 = a - hi.astype(F32)
    mid = r1.astype(BF16)
    lo = (r1 - mid.astype(F32)).astype(BF16)
    return hi, mid, lo


def _dot(a, b):
    return jnp.dot(a, b, preferred_element_type=F32)


def _dot_nt(a, b):
    return lax.dot_general(a, b, (((1,), (1,)), ((), ())), preferred_element_type=F32)


def _dot_tn(a, b):
    return lax.dot_general(a, b, (((0,), (0,)), ((), ())), preferred_element_type=F32)


def _dot3_right(a_f32, t_bf16):
    hi, mid, lo = _split3(a_f32)
    return _dot(hi, t_bf16) + _dot(mid, t_bf16) + _dot(lo, t_bf16)


def _dot3_left(t_bf16, a_f32):
    hi, mid, lo = _split3(a_f32)
    return _dot(t_bf16, hi) + _dot(t_bf16, mid) + _dot(t_bf16, lo)


def _sigmoid(x):
    return 1.0 / (1.0 + jnp.exp(-x))


def _silu(x):
    return x * _sigmoid(x)


def _log_sigmoid(x):
    return jnp.minimum(x, 0.0) - jnp.log(1.0 + jnp.exp(-jnp.abs(x)))


def _rms_rows(x, g):
    return x * lax.rsqrt(jnp.mean(x * x, axis=-1, keepdims=True) + EPS) * g


def _ada_kernel(c_ref, w_ref, b_ref, o_ref):
    a = _silu(c_ref[...]).astype(BF16)
    o_ref[...] = _dot(a, w_ref[...].astype(BF16)) + b_ref[...]


def _ada_call(cc, w_ada, b_ada):
    nb, d = cc.shape
    n = w_ada.shape[1]
    tn = 1536
    return pl.pallas_call(
        _ada_kernel,
        grid=(n // tn,),
        in_specs=[pl.BlockSpec((nb, d), lambda j: (0, 0)),
                  pl.BlockSpec((d, tn), lambda j: (0, j)),
                  pl.BlockSpec((1, tn), lambda j: (0, j))],
        out_specs=pl.BlockSpec((nb, tn), lambda j: (0, j)),
        out_shape=jax.ShapeDtypeStruct((nb, n), F32),
        compiler_params=_cparams(("parallel",)),
        name="ada",
    )(cc, w_ada, b_ada)


def _head_norm(x, g, bd):
    ss = _dot3_right(x * x, bd)
    return x * lax.rsqrt(ss * (1.0 / ATTN_HEAD_DIM) + EPS) * g


def _rope(x, cos, sin_signed):
    w = x.shape[1]
    q = ATTN_HEAD_DIM // 4
    lane = lax.broadcasted_iota(I32, x.shape, 1)
    first = (lane % (2 * q)) < q
    partner = jnp.where(first, pltpu.roll(x, w - q, 1), pltpu.roll(x, q, 1))
    return x * cos + partner * sin_signed


def _inproj_kernel(is_ctx, *refs):
    if is_ctx:
        (x_ref, sh_ref, sc_ref, gpre_ref, wkv_ref, wm_ref, wif_ref, wift_ref, bif_ref, bift_ref,
         gk_ref, bd_ref, k_out, v_out, mqk_out, mv_out, gc_out, gt_out) = refs
    else:
        (x_ref, sh_ref, sc_ref, gpre_ref, wkv_ref, wm_ref, wif_ref, wift_ref, bif_ref, bift_ref,
         gk_ref, bd_ref, wq_ref, wo_ref, wg_ref, gq_ref, cos_ref, sin_ref,
         k_out, v_out, mqk_out, mv_out, gc_out, gt_out, q_out, mo_out, gg_out) = refs
    x = x_ref[...]
    h = _rms_rows(x, gpre_ref[...])
    h = h * (1.0 + sc_ref[...]) + sh_ref[...]
    hb = h.astype(BF16)

    kv = _dot(hb, wkv_ref[...])
    k = _head_norm(kv[:, :KV_WIDTH], gk_ref[...], bd_ref[:KV_WIDTH, :KV_WIDTH])
    if not is_ctx:
        k = _rope(k, cos_ref[:, :KV_WIDTH], sin_ref[:, :KV_WIDTH])
    k_out[...] = k.astype(BF16)
    v_out[...] = kv[:, KV_WIDTH:].astype(BF16)

    m = _dot(hb, wm_ref[...])
    mqk_out[...] = m[:, :2 * MLSTM_WIDTH].astype(BF16)
    mv_out[...] = m[:, 2 * MLSTM_WIDTH:].astype(BF16)

    gc = _dot(hb, wif_ref[...]) + bif_ref[...]
    lane = lax.broadcasted_iota(I32, gc.shape, 1)
    gc_out[...] = jnp.where(lane < 2 * MLSTM_HEADS, gc, _log_sigmoid(gc))
    gt = _dot_nt(wift_ref[...], hb) + bift_ref[...]
    row = lax.broadcasted_iota(I32, gt.shape, 0)
    gt_out[...] = jnp.where(row < 2 * MLSTM_HEADS, gt, _log_sigmoid(gt))

    if not is_ctx:
        q = _dot(hb, wq_ref[...])
        q = _head_norm(q, gq_ref[...], bd_ref[...])
        q = _rope(q, cos_ref[...], sin_ref[...]) * (ATTN_HEAD_DIM ** -0.5)
        q_out[...] = q.astype(BF16)
        mo_out[...] = _dot(hb, wo_ref[...]).astype(BF16)
        gg_out[...] = _dot(hb, wg_ref[...]).astype(BF16)


def _inproj_call(is_ctx, xin, mod4, mod_row_of_batch, gpre, wts, tables, tm):
    B, n, D = xin.shape
    ns = n // tm
    grid = (ns, B)

    def tok(w):
        return pl.BlockSpec((None, tm, w), lambda s, b: (b, s, 0))

    def full(a):
        nd = a.ndim
        return pl.BlockSpec(a.shape, lambda s, b: (0,) * nd)

    def modspec(j):
        return pl.BlockSpec((None, None, 1, D), lambda s, b: (mod_row_of_batch(b), j, 0, 0))

    common = [wts["wkv"], wts["wm"], wts["wif"], wts["wift"], wts["bif"], wts["bift"], wts["gk"], wts["bd"]]
    in_specs = [tok(D), modspec(0), modspec(1), full(gpre)] + [full(a) for a in common]
    args = [xin, mod4, mod4, gpre] + common
    out_shape = [jax.ShapeDtypeStruct((B, n, KV_WIDTH), BF16),
                 jax.ShapeDtypeStruct((B, n, KV_WIDTH), BF16),
                 jax.ShapeDtypeStruct((B, n, 2 * MLSTM_WIDTH), BF16),
                 jax.ShapeDtypeStruct((B, n, MLSTM_WIDTH), BF16),
                 jax.ShapeDtypeStruct((B, n, LANES), F32),
                 jax.ShapeDtypeStruct((B, 4 * MLSTM_HEADS, n), F32)]
    out_specs = [tok(KV_WIDTH), tok(KV_WIDTH), tok(2 * MLSTM_WIDTH), tok(MLSTM_WIDTH), tok(LANES),
                 pl.BlockSpec((None, 4 * MLSTM_HEADS, tm), lambda s, b: (b, 0, s))]
    names = ["k", "v", "mqk", "mv", "gc", "gt"]
    if not is_ctx:
        extra = [wts["wq"], wts["wo"], wts["wg"], wts["gq"]]
        in_specs += [full(a) for a in extra]
        args += extra
        cos, sin = tables
        in_specs += [pl.BlockSpec((tm, ATTN_WIDTH), lambda s, b: (s, 0))] * 2
        args += [cos, sin]
        out_shape += [jax.ShapeDtypeStruct((B, n, ATTN_WIDTH), BF16),
                      jax.ShapeDtypeStruct((B, n, MLSTM_WIDTH), BF16),
                      jax.ShapeDtypeStruct((B, n, 2 * D), BF16)]
        out_specs += [tok(ATTN_WIDTH), tok(MLSTM_WIDTH), tok(2 * D)]
        names += ["q", "mo", "gg"]
    outs = pl.pallas_call(
        functools.partial(_inproj_kernel, is_ctx),
        grid=grid, in_specs=in_specs, out_specs=out_specs, out_shape=out_shape,
        compiler_params=_cparams(("parallel", "parallel")),
        name="inproj_ctx" if is_ctx else "inproj",
    )(*args)
    return dict(zip(names, outs))


def _attn_kernel(q_ref, kc_ref, kx_ref, vc_ref, vx_ref, o_ref):
    kc, kx, vc, vx = kc_ref[...], kx_ref[...], vc_ref[...], vx_ref[...]
    tq = q_ref.shape[0]
    lane = lax.broadcasted_iota(I32, (tq, LANES), 1)
    low = lane < ATTN_HEAD_DIM
    n_slab = ATTN_WIDTH // LANES
    for j in range(n_slab):
        slab = q_ref[:, j * LANES:(j + 1) * LANES]
        zs = []
        for part in range(2):
            qh = jnp.where(low if part == 0 else jnp.logical_not(low), slab, jnp.zeros_like(slab))
            sc = _dot_nt(qh, kc)
            sx = _dot_nt(qh, kx)
            m = jnp.maximum(jnp.max(sc, axis=1, keepdims=True), jnp.max(sx, axis=1, keepdims=True))
            pc = jnp.exp(sc - m)
            px = jnp.exp(sx - m)
            l = jnp.sum(pc, axis=1, keepdims=True) + jnp.sum(px, axis=1, keepdims=True)
            z = _dot(pc.astype(BF16), vc) + _dot(px.astype(BF16), vx)
            zs.append(z / l)
        o_ref[:, j * LANES:(j + 1) * LANES] = jnp.where(low, zs[0], zs[1]).astype(BF16)


def _attn_call(q, kc, kx, vc, vx, tq):
    B, S, _ = q.shape
    CL = kc.shape[1]

    def seq(n):
        return pl.BlockSpec((None, n, KV_WIDTH), lambda b, s: (b, 0, 0))

    return pl.pallas_call(
        _attn_kernel,
        grid=(B, S // tq),
        in_specs=[pl.BlockSpec((None, tq, ATTN_WIDTH), lambda b, s: (b, s, 0)),
                  seq(CL), seq(S), seq(CL), seq(S)],
        out_specs=pl.BlockSpec((None, tq, ATTN_WIDTH), lambda b, s: (b, s, 0)),
        out_shape=jax.ShapeDtypeStruct((B, S, ATTN_WIDTH), BF16),
        compiler_params=_cparams(("parallel", "parallel")),
        name="attn",
    )(q, kc, kx, vc, vx)


def _mlstm_kernel(n_chunks_x, n_chunks_c,
                  mqk_x, mv_x, gc_x, gt_x, mqk_c, mv_c, gc_c, gt_c, convw, convb, tl_ref, tu_ref,
                  hf_out, hb_out, qs_x, ks_x, qs_c, ks_c, ct_ref, m_ref):
    L = MLSTM_L
    H = MLSTM_HEADS
    W = MLSTM_WIDTH
    hd = MLSTM_HEAD_DIM
    tl = tl_ref[...]
    tu = tu_ref[...]
    w0, w1, w2, cb = convw[0:1, :], convw[1:2, :], convw[2:3, :], convb[...]
    rows = lax.broadcasted_iota(I32, (L, 2 * W), 0)
    lane2 = lax.broadcasted_iota(I32, (L, 2 * W), 1)
    edge = 2 * SUBLANES

    def conv_chunk(src, nch, qdst, kdst, i):
        start = pl.multiple_of(i * L, L)
        a = src[pl.ds(start, L), :].astype(F32)
        pstart = pl.multiple_of(jnp.maximum(start - edge, 0), edge)
        prev_row = src[pl.ds(pstart, edge), :][edge - 1:edge, :].astype(F32)
        prev_row = jnp.where(i > 0, prev_row, jnp.zeros_like(prev_row))
        nstart = pl.multiple_of(jnp.minimum(start + L, (nch - 1) * L), edge)
        next_row = src[pl.ds(nstart, edge), :][0:1, :].astype(F32)
        next_row = jnp.where(i < nch - 1, next_row, jnp.zeros_like(next_row))
        a_prev = jnp.where(rows == 0, prev_row, pltpu.roll(a, 1, 0))
        a_next = jnp.where(rows == L - 1, next_row, pltpu.roll(a, L - 1, 0))
        y = _silu(w0 * a_prev + w1 * a + w2 * a_next + cb)
        y = jnp.where(lane2 < W, y * (hd ** -0.5), y)
        qdst[pl.ds(start, L), :] = y[:, :W].astype(BF16)
        kdst[pl.ds(start, L), :] = y[:, W:].astype(BF16)

    def conv_c(i, carry):
        conv_chunk(mqk_c, n_chunks_c, qs_c, ks_c, i)
        return carry

    def conv_x(i, carry):
        conv_chunk(mqk_x, n_chunks_x, qs_x, ks_x, i)
        return carry

    lax.fori_loop(0, n_chunks_c, conv_c, 0)
    lax.fori_loop(0, n_chunks_x, conv_x, 0)

    ct_ref[...] = jnp.zeros_like(ct_ref)
    m_ref[...] = jnp.zeros_like(m_ref)

    ri = lax.broadcasted_iota(I32, (L, L), 0)
    ci = lax.broadcasted_iota(I32, (L, L), 1)
    vlane = lax.broadcasted_iota(I32, (L, hd), 1)
    ones_col = jnp.where(vlane == 0, 1.0, 0.0).astype(BF16)

    def chunk_step(qs, ks, mv, gc_ref, gt_ref, start, direction, h_out):
        gc = gc_ref[pl.ds(start, L), :]
        gt = gt_ref[:, pl.ds(start, L)]
        if direction == 0:
            bcol = _dot3_left(tl, gc)
            brow = _dot3_right(gt, tu)
            mask = ci <= ri
        else:
            bcol = _dot3_left(tu, gc)
            brow = _dot3_right(gt, tl)
            mask = ci >= ri
        for hh in range(H):
            c = direction * H + hh
            q_c = qs[pl.ds(start, L), hh * hd:(hh + 1) * hd]
            k_c = ks[pl.ds(start, L), hh * hd:(hh + 1) * hd]
            v_c = mv[pl.ds(start, L), hh * hd:(hh + 1) * hd]
            vaug = jnp.concatenate([v_c, ones_col], axis=1)
            li_col = gc[:, c:c + 1]
            li_row = gt[c:c + 1, :]
            b_col = bcol[:, 2 * H + c:2 * H + c + 1]
            b_row = brow[2 * H + c:2 * H + c + 1, :]
            if direction == 0:
                b_tot = b_row[:, L - 1:L]
            else:
                b_tot = b_row[:, 0:1]
            m_old = m_ref[c:c + 1, 0:1]
            ct = ct_ref[c]
            if h_out is not None:
                dm = jnp.where(mask, b_col - b_row + li_row, NEG_INF)
                inter = b_col + m_old
                m_q = jnp.maximum(inter, jnp.max(dm, axis=1, keepdims=True))
                wmat = jnp.exp(dm - m_q)
                w_inter = jnp.exp(inter - m_q)
                p = (_dot_nt(q_c, k_c) * wmat).astype(BF16)
                r = w_inter * _dot(q_c, ct.astype(BF16)) + _dot(p, vaug)
                den = jnp.maximum(jnp.abs(r[:, hd:hd + 1]), jnp.exp(-m_q))
                h_out[pl.ds(start, L), hh * hd:(hh + 1) * hd] = (r[:, :hd] / den).astype(BF16)
            dec = b_tot - b_col + li_col
            m_new = jnp.maximum(b_tot + m_old, jnp.max(dec, axis=0, keepdims=True))
            w_s = jnp.exp(dec - m_new)
            w_old = jnp.exp(b_tot + m_old - m_new)
            kw = (k_c.astype(F32) * w_s).astype(BF16)
            ct_ref[c] = w_old * ct + _dot_tn(kw, vaug)
            m_ref[c:c + 1, :] = jnp.broadcast_to(m_new, (1, LANES))

    def ctx_body(i, carry):
        chunk_step(qs_c, ks_c, mv_c, gc_c, gt_c, pl.multiple_of(i * L, L), 0, None)
        chunk_step(qs_c, ks_c, mv_c, gc_c, gt_c, pl.multiple_of((n_chunks_c - 1 - i) * L, L), 1, None)
        return carry

    lax.fori_loop(0, n_chunks_c, ctx_body, 0)

    def x_body(i, carry):
        chunk_step(qs_x, ks_x, mv_x, gc_x, gt_x, pl.multiple_of(i * L, L), 0, hf_out)
        chunk_step(qs_x, ks_x, mv_x, gc_x, gt_x, pl.multiple_of((n_chunks_x - 1 - i) * L, L), 1, hb_out)
        return carry

    lax.fori_loop(0, n_chunks_x, x_body, 0)


def _mlstm_call(px, pc, convw, convb):
    B, S, _ = px["mqk"].shape
    CL = pc["mqk"].shape[1]
    L = MLSTM_L
    W = MLSTM_WIDTH
    r = np.arange(L)
    tl = jnp.asarray(r[None, :] <= r[:, None], dtype=BF16)
    tu = jnp.asarray(r[None, :] >= r[:, None], dtype=BF16)

    def seq(n, w):
        return pl.BlockSpec((None, n, w), lambda b: (b, 0, 0))

    def gts(n):
        return pl.BlockSpec((None, 4 * MLSTM_HEADS, n), lambda b: (b, 0, 0))

    def full(a):
        nd = a.ndim
        return pl.BlockSpec(a.shape, lambda b: (0,) * nd)

    return pl.pallas_call(
        functools.partial(_mlstm_kernel, S // L, CL // L),
        grid=(B,),
        in_specs=[seq(S, 2 * W), seq(S, W), seq(S, LANES), gts(S),
                  seq(CL, 2 * W), seq(CL, W), seq(CL, LANES), gts(CL),
                  full(convw), full(convb), full(tl), full(tu)],
        out_specs=[seq(S, W), seq(S, W)],
        out_shape=[jax.ShapeDtypeStruct((B, S, W), BF16)] * 2,
        scratch_shapes=[pltpu.VMEM((S, W), BF16), pltpu.VMEM((S, W), BF16),
                        pltpu.VMEM((CL, W), BF16), pltpu.VMEM((CL, W), BF16),
                        pltpu.VMEM((2 * MLSTM_HEADS, MLSTM_HEAD_DIM, 2 * MLSTM_HEAD_DIM), F32),
                        pltpu.VMEM((2 * MLSTM_HEADS, LANES), F32)],
        compiler_params=_cparams(("parallel",)),
        name="mlstm",
    )(px["mqk"], px["mv"], px["gc"], px["gt"], pc["mqk"], pc["mv"], pc["gc"], pc["gt"], convw, convb, tl, tu)


def _merge_kernel(x_ref, a_ref, hf_ref, hb_ref, mo_ref, gg_ref, g1_ref, sh2_ref, sc2_ref,
                  gml_ref, gpost_ref, gpre_ref, wba_ref, wbm_ref, wout_ref, wrt_ref,
                  x1_out, fin_out, st_out):
    D = D_MODEL
    hs = hf_ref[...].astype(F32) + hb_ref[...].astype(F32)
    gml = gml_ref[...]
    parts = []
    for hh in range(MLSTM_HEADS):
        sl = slice(hh * MLSTM_HEAD_DIM, (hh + 1) * MLSTM_HEAD_DIM)
        parts.append(_rms_rows(hs[:, sl], gml[:, sl]))
    hn = jnp.concatenate(parts, axis=1)
    m_out = (hn * _sigmoid(mo_ref[...].astype(F32))).astype(BF16)
    gg = gg_ref[...].astype(F32)
    y = _sigmoid(gg[:, :D]) * _dot(a_ref[...], wba_ref[...]) + _sigmoid(gg[:, D:]) * _dot(m_out, wbm_ref[...])
    yx = _dot(y.astype(BF16), wout_ref[...])
    x1 = x_ref[...] + g1_ref[...] * _rms_rows(yx, gpost_ref[...])
    x1_out[...] = x1
    fin = _rms_rows(x1, gpre_ref[...]) * (1.0 + sc2_ref[...]) + sh2_ref[...]
    fin_out[...] = fin
    st_out[...] = _sigmoid(_dot_nt(wrt_ref[...], fin.astype(BF16)))


def _merge_call(x, a_out, hf, hb, mo, gg, mod4, wts, tm):
    B, S, D = x.shape
    ns = S // tm

    def tok(w):
        return pl.BlockSpec((None, tm, w), lambda s, b: (b, s, 0))

    def full(a):
        nd = a.ndim
        return pl.BlockSpec(a.shape, lambda s, b: (0,) * nd)

    def modspec(j):
        return pl.BlockSpec((None, None, 1, D), lambda s, b: (b, j, 0, 0))

    consts = [wts["gml"], wts["gpost"], wts["gpre2"], wts["wba"], wts["wbm"], wts["wout"], wts["wrt"]]
    return pl.pallas_call(
        _merge_kernel,
        grid=(ns, B),
        in_specs=[tok(D), tok(ATTN_WIDTH), tok(MLSTM_WIDTH), tok(MLSTM_WIDTH), tok(MLSTM_WIDTH), tok(2 * D),
                  modspec(2), modspec(3), modspec(4)] + [full(a) for a in consts],
        out_specs=[tok(D), tok(D), pl.BlockSpec((N_EXPERTS, tm), lambda s, b: (0, b * ns + s))],
        out_shape=[jax.ShapeDtypeStruct((B, S, D), F32), jax.ShapeDtypeStruct((B, S, D), F32),
                   jax.ShapeDtypeStruct((N_EXPERTS, B * S), F32)],
        compiler_params=_cparams(("parallel", "parallel")),
        name="merge",
    )(x, a_out, hf, hb, mo, gg, mod4, mod4, mod4, *consts)


def _first_argmax(v, idx, big):
    m = jnp.max(v, axis=0, keepdims=True)
    first = jnp.min(jnp.where(v == m, idx, big), axis=0, keepdims=True)
    return m, first


def _route_kernel(st_ref, bias_ref, eidx_out, w_out, cnt_out):
    s = st_ref[...]
    E, tn = s.shape
    sel = s + bias_ref[...]
    gi = lax.broadcasted_iota(I32, (GROUP_SIZE, tn), 0)
    gscores = []
    for g in range(N_GROUPS):
        sg = sel[g * GROUP_SIZE:(g + 1) * GROUP_SIZE, :]
        m1, i1 = _first_argmax(sg, gi, GROUP_SIZE)
        m2 = jnp.max(jnp.where(gi == i1, NEG_INF, sg), axis=0, keepdims=True)
        gscores.append(m1 + m2)
    gs = jnp.concatenate(gscores, axis=0)
    gidx = lax.broadcasted_iota(I32, (N_GROUPS, tn), 0)
    chosen = jnp.zeros((N_GROUPS, tn), dtype=F32)
    for _ in range(TOPK_GROUPS):
        _, ig = _first_argmax(gs, gidx, N_GROUPS)
        hit = gidx == ig
        chosen = jnp.where(hit, 1.0, chosen)
        gs = jnp.where(hit, NEG_INF, gs)
    masked = jnp.concatenate(
        [jnp.where(chosen[g:g + 1, :] > 0.0, sel[g * GROUP_SIZE:(g + 1) * GROUP_SIZE, :], NEG_INF)
         for g in range(N_GROUPS)], axis=0)
    ei = lax.broadcasted_iota(I32, (E, tn), 0)
    idxs, ws = [], []
    member = jnp.zeros((E, tn), dtype=F32)
    for _ in range(TOP_K):
        _, ie = _first_argmax(masked, ei, E)
        hit = ei == ie
        ws.append(jnp.sum(jnp.where(hit, s, 0.0), axis=0, keepdims=True))
        idxs.append(ie)
        member = jnp.where(hit, 1.0, member)
        masked = jnp.where(hit, NEG_INF, masked)
    w = jnp.concatenate(ws, axis=0)
    w = w / jnp.sum(w, axis=0, keepdims=True) * ROUTED_SCALE
    eidx_out[...] = jnp.concatenate(idxs, axis=0)
    w_out[...] = w

    @pl.when(pl.program_id(0) == 0)
    def _():
        cnt_out[...] = jnp.zeros_like(cnt_out)

    cnt_out[...] += jnp.broadcast_to(jnp.sum(member, axis=1, keepdims=True), cnt_out.shape)


def _route_call(st, e_bias_col, tn):
    E, T = st.shape
    return pl.pallas_call(
        _route_kernel,
        grid=(T // tn,),
        in_specs=[pl.BlockSpec((E, tn), lambda i: (0, i)), pl.BlockSpec((E, 1), lambda i: (0, 0))],
        out_specs=[pl.BlockSpec((TOP_K, tn), lambda i: (0, i)), pl.BlockSpec((TOP_K, tn), lambda i: (0, i)),
                   pl.BlockSpec((E, LANES), lambda i: (0, 0))],
        out_shape=[jax.ShapeDtypeStruct((TOP_K, T), I32), jax.ShapeDtypeStruct((TOP_K, T), F32),
                   jax.ShapeDtypeStruct((E, LANES), F32)],
        compiler_params=_cparams(("arbitrary",)),
        name="route",
    )(st, e_bias_col)


def _pos_kernel(eidx_ref, pstart_ref, su_ref, pos_out, run_ref):
    @pl.when(pl.program_id(0) == 0)
    def _():
        run_ref[...] = jnp.zeros_like(run_ref)

    eidx = eidx_ref[...]
    tn = eidx.shape[1]
    E = N_EXPERTS
    ei = lax.broadcasted_iota(I32, (E, tn), 0)
    member = jnp.zeros((E, tn), dtype=F32)
    for k in range(TOP_K):
        member = jnp.where(ei == eidx[k:k + 1, :], 1.0, member)
    prefix = _dot(member.astype(BF16), su_ref[...])
    slot = prefix + (pstart_ref[...] + run_ref[:, 0:1])
    rows = [jnp.sum(jnp.where(ei == eidx[k:k + 1, :], slot, 0.0), axis=0, keepdims=True) for k in range(TOP_K)]
    pos_out[...] = jnp.concatenate(rows, axis=0).astype(I32)
    run_ref[...] += jnp.broadcast_to(jnp.sum(member, axis=1, keepdims=True), run_ref.shape)


def _pos_call(eidx, pstart_col, tn):
    K, T = eidx.shape
    r = np.arange(tn)
    su = jnp.asarray(r[:, None] < r[None, :], dtype=BF16)
    return pl.pallas_call(
        _pos_kernel,
        grid=(T // tn,),
        in_specs=[pl.BlockSpec((K, tn), lambda i: (0, i)), pl.BlockSpec((N_EXPERTS, 1), lambda i: (0, 0)),
                  pl.BlockSpec((tn, tn), lambda i: (0, 0))],
        out_specs=pl.BlockSpec((K, tn), lambda i: (0, i)),
        out_shape=jax.ShapeDtypeStruct((K, T), I32),
        scratch_shapes=[pltpu.VMEM((N_EXPERTS, LANES), F32)],
        compiler_params=_cparams(("arbitrary",)),
        name="pos",
    )(eidx, pstart_col, su)


def _dispatch_kernel(td, pos_ref, fin_ref, xs_in, xs_out, sem):
    del xs_in

    def row_copy(t, slot):
        return pltpu.make_async_copy(fin_ref.at[pl.ds(t, 1), :], xs_out.at[pl.ds(slot, 1), :], sem)

    def issue(t, carry):
        for k in range(TOP_K):
            row_copy(t, pos_ref[k, t]).start()
        return carry

    lax.fori_loop(0, td, issue, 0)

    whole = xs_out.at[pl.ds(0, td * TOP_K), :]
    pltpu.make_async_copy(whole, whole, sem).wait()


def _dispatch_call(pos, fin2, xs_zero, td):
    K, T = pos.shape
    D = fin2.shape[1]
    return pl.pallas_call(
        functools.partial(_dispatch_kernel, td),
        grid=(T // td,),
        in_specs=[pl.BlockSpec((K, td), lambda i: (0, i), memory_space=pltpu.SMEM),
                  pl.BlockSpec((td, D), lambda i: (i, 0)),
                  pl.BlockSpec(memory_space=pl.ANY)],
        out_specs=pl.BlockSpec(memory_space=pl.ANY),
        out_shape=jax.ShapeDtypeStruct(xs_zero.shape, xs_zero.dtype),
        scratch_shapes=[pltpu.SemaphoreType.DMA(())],
        input_output_aliases={2: 0},
        compiler_params=_cparams(("arbitrary",)),
        name="dispatch",
    )(pos, fin2, xs_zero)


def _experts_kernel(be_ref, nu_ref, xs_ref, wg_ref, wu_ref, wd_ref, ys_ref, wgu_s, wd_s):
    i = pl.program_id(0)
    prev = be_ref[jnp.maximum(i - 1, 0)]
    changed = jnp.logical_or(i == 0, be_ref[i] != prev)

    @pl.when(jnp.logical_and(changed, i < nu_ref[0]))
    def _():
        wgu_s[:, :EXPERT_FF] = wg_ref[...].astype(BF16)
        wgu_s[:, EXPERT_FF:] = wu_ref[...].astype(BF16)
        wd_s[...] = wd_ref[...].astype(BF16)

    @pl.when(i < nu_ref[0])
    def _():
        xb = xs_ref[...].astype(BF16)
        gu = _dot(xb, wgu_s[...])
        hid = (_silu(gu[:, :EXPERT_FF]) * gu[:, EXPERT_FF:]).astype(BF16)
        ys_ref[...] = _dot(hid, wd_s[...])

    @pl.when(i >= nu_ref[0])
    def _():
        ys_ref[...] = jnp.zeros_like(ys_ref)


def _experts_call(block_e, n_used, xs, wg, wu, wd):
    P, D = xs.shape
    bm = EXPERT_BM
    nb = P // bm

    def xmap(i, be, nu):
        return (jnp.minimum(i, nu[0] - 1), 0)

    def wmap(i, be, nu):
        return (be[i], 0, 0)

    grid_spec = pltpu.PrefetchScalarGridSpec(
        num_scalar_prefetch=2,
        grid=(nb,),
        in_specs=[pl.BlockSpec((bm, D), xmap),
                  pl.BlockSpec((None, D, EXPERT_FF), wmap),
                  pl.BlockSpec((None, D, EXPERT_FF), wmap),
                  pl.BlockSpec((None, EXPERT_FF, D), wmap)],
        out_specs=pl.BlockSpec((bm, D), lambda i, be, nu: (i, 0)),
        scratch_shapes=[pltpu.VMEM((D, 2 * EXPERT_FF), BF16), pltpu.VMEM((EXPERT_FF, D), BF16)],
    )
    return pl.pallas_call(
        _experts_kernel,
        grid_spec=grid_spec,
        out_shape=jax.ShapeDtypeStruct((P, D), F32),
        compiler_params=_cparams(("arbitrary",)),
        name="experts",
    )(block_e, n_used, xs, wg, wu, wd)


def _final_kernel(tf, pos_cur, pos_nxt, ys_hbm, w_ref, fin_ref, x1_ref, g2_ref, gpost_ref,
                  wsg_ref, wsu_ref, wsd_ref, o_ref, gbuf, sem):
    i = pl.program_id(0)
    n = pl.num_programs(0)
    slot = i % 2

    def row_copy(pos_ref, t, k, sl):
        return pltpu.make_async_copy(ys_hbm.at[pl.ds(pos_ref[k, t], 1), :],
                                     gbuf.at[sl, k, pl.ds(t, 1), :], sem.at[sl])

    def issue_from(pos_ref, sl):
        def body(t, carry):
            for k in range(TOP_K):
                row_copy(pos_ref, t, k, sl).start()
            return carry
        lax.fori_loop(0, tf, body, 0)

    @pl.when(i == 0)
    def _():
        issue_from(pos_cur, 0)

    def step(cur):
        @pl.when(i + 1 < n)
        def _():
            issue_from(pos_nxt, 1 - cur)

        pltpu.make_async_copy(gbuf.at[cur], gbuf.at[cur], sem.at[cur]).wait()

        w = w_ref[...]
        eye = lax.broadcasted_iota(I32, (tf, tf), 0) == lax.broadcasted_iota(I32, (tf, tf), 1)
        acc = jnp.zeros((tf, D_MODEL), dtype=F32)
        for k in range(TOP_K):
            w_col = jnp.sum(jnp.where(eye, w[k:k + 1, :], 0.0), axis=1, keepdims=True)
            acc = acc + w_col * gbuf[cur, k]
        fb = fin_ref[...].astype(BF16)
        hid = (_silu(_dot(fb, wsg_ref[...])) * _dot(fb, wsu_ref[...])).astype(BF16)
        fx = acc + _dot(hid, wsd_ref[...])
        o_ref[...] = x1_ref[...] + g2_ref[...] * _rms_rows(fx, gpost_ref[...])

    @pl.when(slot == 0)
    def _():
        step(0)

    @pl.when(slot == 1)
    def _():
        step(1)


def _final_call(pos, ys, w, fin2, x1f, mod4, gpost, wsg, wsu, wsd, S, tf):
    K, T = pos.shape
    D = D_MODEL
    n = T // tf
    per_b = S // tf

    def full(a):
        nd = a.ndim
        return pl.BlockSpec(a.shape, lambda i: (0,) * nd)

    return pl.pallas_call(
        functools.partial(_final_kernel, tf),
        grid=(n,),
        in_specs=[pl.BlockSpec((K, tf), lambda i: (0, i), memory_space=pltpu.SMEM),
                  pl.BlockSpec((K, tf), lambda i: (0, jnp.minimum(i + 1, n - 1)), memory_space=pltpu.SMEM),
                  pl.BlockSpec(memory_space=pl.ANY),
                  pl.BlockSpec((K, tf), lambda i: (0, i)),
                  pl.BlockSpec((tf, D), lambda i: (i, 0)),
                  pl.BlockSpec((tf, D), lambda i: (i, 0)),
                  pl.BlockSpec((None, None, 1, D), lambda i: (i // per_b, 5, 0, 0)),
                  full(gpost), full(wsg), full(wsu), full(wsd)],
        out_specs=pl.BlockSpec((tf, D), lambda i: (i, 0)),
        out_shape=jax.ShapeDtypeStruct((T, D), F32),
        scratch_shapes=[pltpu.VMEM((2, K, tf, D), F32), pltpu.SemaphoreType.DMA((2,))],
        compiler_params=_cparams(("arbitrary",)),
        name="final",
    )(pos, pos, ys, w, fin2, x1f, mod4, gpost, wsg, wsu, wsd)


def _rope_tables(S):
    nf = ATTN_HEAD_DIM // 4
    t = jnp.arange(S, dtype=jnp.int32)
    pos = jnp.stack([t // GRID_W, t % GRID_W], axis=-1).astype(F32)
    inv_freq = ROPE_THETA ** (-jnp.arange(nf, dtype=F32) / nf)
    ang = pos[:, :, None] * inv_freq
    cos, sin = jnp.cos(ang), jnp.sin(ang)
    cos_h = jnp.concatenate([cos, cos], axis=-1).reshape(S, ATTN_HEAD_DIM)
    sin_h = jnp.concatenate([-sin, sin], axis=-1).reshape(S, ATTN_HEAD_DIM)
    return jnp.tile(cos_h, (1, ATTN_HEADS)), jnp.tile(sin_h, (1, ATTN_HEADS))


def _pick_tile(n, pref):
    t = min(n, pref)
    while n % t:
        t //= 2
    return t


def _layer(x, ctx, cc, l, p):
    B, S, D = x.shape
    CL = ctx.shape[1]
    T = B * S
    NBp = cc.shape[0]

    mod = _ada_call(cc, p["w_ada"][l], p["b_ada"][l][None, :])
    mod4 = mod.reshape(NBp, N_MOD, 1, D)

    w_in = p["w_in"][l]
    o = np.cumsum((0, ATTN_WIDTH, KV_WIDTH, KV_WIDTH, MLSTM_WIDTH, MLSTM_WIDTH, MLSTM_WIDTH, MLSTM_WIDTH,
                   2 * MLSTM_HEADS, 2 * MLSTM_HEADS, D, D))
    half = ATTN_HEADS // 2
    head_order = np.stack([np.arange(half), np.arange(half) + half], axis=1).reshape(-1)
    qperm = (head_order[:, None] * ATTN_HEAD_DIM + np.arange(ATTN_HEAD_DIM)[None, :]).reshape(-1)
    wq = w_in[:, o[0]:o[1]][:, qperm].astype(BF16)
    wkv = w_in[:, o[1]:o[3]].astype(BF16)
    wm = w_in[:, o[3]:o[6]].astype(BF16)
    wo = w_in[:, o[6]:o[7]].astype(BF16)
    w_gates = w_in[:, o[7]:o[9]]
    wif = jnp.pad(w_gates, ((0, 0), (0, LANES - 4 * MLSTM_HEADS))).astype(BF16)
    wift = w_gates.T.astype(BF16)
    wg = w_in[:, o[9]:o[11]].astype(BF16)
    b_gates = jnp.concatenate([p["b_igate"][l], p["b_fgate"][l]]).astype(F32)
    bif = jnp.pad(b_gates, (0, LANES - 4 * MLSTM_HEADS))[None, :]
    bift = b_gates[:, None]
    hidx = np.arange(ATTN_WIDTH) // ATTN_HEAD_DIM
    bd = jnp.asarray(hidx[:, None] == hidx[None, :], dtype=BF16)
    wts = dict(wq=wq, wkv=wkv, wm=wm, wo=wo, wg=wg, wif=wif, wift=wift, bif=bif, bift=bift,
               gq=jnp.tile(p["g_q"][l], ATTN_HEADS)[None, :], gk=jnp.tile(p["g_k"][l], ATTN_KV_HEADS)[None, :],
               bd=bd)
    gpre = p["g_pre_mix"][l][None, :]

    tm = _pick_tile(S, 256)
    px = _inproj_call(False, x, mod4, lambda b: b, gpre, wts, _rope_tables(S), tm)
    pc = _inproj_call(True, ctx, mod4, lambda b: B, gpre, wts, None, _pick_tile(CL, 256))

    a_out = _attn_call(px["q"], pc["k"], px["k"], pc["v"], px["v"], _pick_tile(S, 256))
    hf, hb = _mlstm_call(px, pc, p["conv_w"][l], p["conv_b"][l][None, :])

    mw = dict(gml=p["g_mlstm"][l][None, :], gpost=p["g_post_mix"][l][None, :], gpre2=p["g_pre_ffn"][l][None, :],
              wba=p["w_br_attn"][l][qperm, :].astype(BF16), wbm=p["w_br_mlstm"][l].astype(BF16),
              wout=p["w_out"][l].astype(BF16), wrt=p["w_router"][l].T.astype(BF16))
    x1, fin, st = _merge_call(x, a_out, hf, hb, px["mo"], px["gg"], mod4, mw, tm)

    tn = _pick_tile(T, 512)
    eidx, w, cnt = _route_call(st, p["e_bias"][l].astype(F32)[:, None], tn)
    counts = cnt[:, 0].astype(I32)
    bm = EXPERT_BM
    padded = (counts + bm - 1) // bm * bm
    pend = jnp.cumsum(padded)
    pstart = pend - padded
    n_blocks = (T * TOP_K + N_EXPERTS * (bm - 1) + bm - 1) // bm
    block_start = jnp.arange(n_blocks, dtype=I32) * bm
    block_e = jnp.minimum(jnp.sum((pend[None, :] <= block_start[:, None]).astype(I32), axis=1), N_EXPERTS - 1)
    n_used = (pend[-1:] // bm).astype(I32)
    pos = _pos_call(eidx, pstart.astype(F32)[:, None], tn)

    fin2 = fin.reshape(T, D)
    xs = _dispatch_call(pos, fin2, jnp.zeros((n_blocks * bm, D), F32), _pick_tile(T, 512))
    ys = _experts_call(block_e, n_used, xs, p["w_exp_gate"][l], p["w_exp_up"][l], p["w_exp_down"][l])
    out = _final_call(pos, ys, w, fin2, x1.reshape(T, D), mod4, p["g_post_ffn"][l][None, :],
                      p["w_sh_gate"][l].astype(BF16), p["w_sh_up"][l].astype(BF16),
                      p["w_sh_down"][l].astype(BF16), S, _pick_tile(S, 128))
    return out.reshape(B, S, D)


def kernel(x, c, ctx, c_ctx, w_ada, b_ada, g_pre_mix, g_post_mix, g_pre_ffn, g_post_ffn, w_in, g_q, g_k, conv_w, conv_b, b_igate, b_fgate, g_mlstm, w_br_attn, w_br_mlstm, w_out, w_router, e_bias, w_exp_gate, w_exp_up, w_exp_down, w_sh_gate, w_sh_up, w_sh_down):
    depth = w_ada.shape[0]
    assert depth == 1, "context-token updates (needed only between layers) are not implemented"
    B = x.shape[0]
    nbp = (B + 1 + SUBLANES - 1) // SUBLANES * SUBLANES
    cc = jnp.concatenate([c, c_ctx[None, :], jnp.zeros((nbp - B - 1, c.shape[1]), c.dtype)], axis=0)
    p = dict(w_ada=w_ada, b_ada=b_ada, g_pre_mix=g_pre_mix, g_post_mix=g_post_mix, g_pre_ffn=g_pre_ffn,
             g_post_ffn=g_post_ffn, w_in=w_in, g_q=g_q, g_k=g_k, conv_w=conv_w, conv_b=conv_b, b_igate=b_igate,
             b_fgate=b_fgate, g_mlstm=g_mlstm, w_br_attn=w_br_attn, w_br_mlstm=w_br_mlstm, w_out=w_out,
             w_router=w_router, e_bias=e_bias, w_exp_gate=w_exp_gate, w_exp_up=w_exp_up, w_exp_down=w_exp_down,
             w_sh_gate=w_sh_gate, w_sh_up=w_sh_up, w_sh_down=w_sh_down)
    return _layer(x, ctx, cc, 0, p)
```
